```python
import math
import jax, jax.numpy as jnp
from jax import lax
import numpy as np

D_MODEL = 1024
BATCH = 32
SEQ = 2048
DEPTH = 2
DEC_BATCH = 128
DEC_SEQ = 8
PAST_LEN = 16384
PAGE_SIZE = 128

MLA_HEADS = 8
MLA_Q_RANK = 384
MLA_KV_RANK = 256
MLA_NOPE = 64
MLA_ROPE = 32
MLA_V = 64
MLA_ROW = MLA_KV_RANK + MLA_ROPE
MLA_W = MLA_HEADS * MLA_V
RET_HEADS = 4
RET_DK = 64
RET_DV = 128
RET_W = RET_HEADS * RET_DV
RET_CHUNK = 128
S5_GROUP = 16
S5_CH = 640
S5_GROUPS = S5_CH // S5_GROUP
S5_STATE = 64
S5_DT_MIN = 1e-3
S5_DT_MAX = 1e-1
SB_HEADS = 6
SB_DIM = 64
SB_W = SB_HEADS * SB_DIM
IN0 = MLA_Q_RANK + MLA_KV_RANK + MLA_ROPE + 2 * RET_HEADS * RET_DK + 2 * RET_W
IN1 = S5_CH + 3 * SB_W
D_FF = 2816
CONV_W = 3
Q_BLOCK = 128
ROPE_THETA = 10000.0
EPS = 1e-6
NEG = -1e30

kernel_name = "hybrid_mla_retention_s5_stickbreak_step"


def _rmsnorm(x, g):
    xf = x.astype(jnp.float32)
    y = xf * lax.rsqrt(jnp.mean(xf * xf, axis=-1, keepdims=True) + EPS) * g.astype(jnp.float32)
    return y.astype(x.dtype)


def _headnorm(o):
    of = o.astype(jnp.float32)
    return (of * lax.rsqrt(jnp.mean(of * of, axis=-1, keepdims=True) + EPS)).astype(o.dtype)


def _rope(x, pos):
    half = x.shape[-1] // 2
    freqs = ROPE_THETA ** (-jnp.arange(half, dtype=jnp.float32) / half)
    ang = pos.astype(jnp.float32)[:, None] * freqs[None, :]
    cos = jnp.cos(ang)[None, :, None, :].astype(x.dtype)
    sin = jnp.sin(ang)[None, :, None, :].astype(x.dtype)
    x1, x2 = x[..., :half], x[..., half:]
    return jnp.concatenate([x1 * cos - x2 * sin, x2 * cos + x1 * sin], axis=-1)


def _map_query_blocks(fn, qs, q_pos):
    lq = q_pos.shape[0]
    bq = Q_BLOCK if lq % Q_BLOCK == 0 else lq
    nb = lq // bq
    qb = tuple(jnp.moveaxis(a.reshape(a.shape[0], nb, bq, *a.shape[2:]), 1, 0) for a in qs)
    out = lax.map(lambda a: fn(a[0], a[1]), (qb, q_pos.reshape(nb, bq)))
    out = jnp.moveaxis(out, 0, 1)
    return out.reshape(out.shape[0], lq, *out.shape[3:])


def _mla_attend(q_full, row_segs, q_pos, k_pos):
    scale = (MLA_NOPE + MLA_ROPE) ** -0.5

    def block(qs, qp):
        (qb,) = qs
        s = jnp.concatenate([jnp.einsum('bqhr,bkr->bhqk', qb, seg) for seg in row_segs], axis=-1)
        s = s.astype(jnp.float32) * scale
        s = jnp.where(k_pos[None, :] <= qp[:, None], s, NEG)
        p = jax.nn.softmax(s, axis=-1).astype(qb.dtype)
        o = None
        off = 0
        for seg in row_segs:
            n = seg.shape[1]
            t = jnp.einsum('bhqk,bkr->bqhr', p[..., off:off + n], seg)
            o = t if o is None else o + t
            off += n
        return o

    return _map_query_blocks(block, (q_full,), q_pos)


def _sb_attend(q, k_segs, v_segs, q_pos, k_pos):
    scale = SB_DIM ** -0.5

    def block(qs, qp):
        (qb,) = qs
        z = jnp.concatenate([jnp.einsum('bqhd,bkhd->bhqk', qb, ks) for ks in k_segs], axis=-1)
        z = z.astype(jnp.float32) * scale
        mask = k_pos[None, :] < qp[:, None]
        log_keep = jnp.where(mask, jax.nn.log_sigmoid(-z), 0.0)
        later = lax.cumsum(log_keep, axis=3, reverse=True) - log_keep
        w = jnp.where(mask, jnp.exp(jax.nn.log_sigmoid(z) + later), 0.0).astype(qb.dtype)
        o = None
        off = 0
        for vs in v_segs:
            n = vs.shape[1]
            t = jnp.einsum('bhqk,bkhd->bqhd', w[..., off:off + n], vs)
            o = t if o is None else o + t
            off += n
        return o

    return _map_query_blocks(block, (q,), q_pos)


def _retention(q, k, v, s0):
    b, l = q.shape[:2]
    c = RET_CHUNK if l % RET_CHUNK == 0 else l
    n = l // c
    log_g = jnp.log(1.0 - 2.0 ** (-5.0 - jnp.arange(RET_HEADS, dtype=jnp.float32)))
    i = jnp.arange(c, dtype=jnp.float32)
    diff = i[:, None] - i[None, :]
    intra = jnp.where(diff >= 0, jnp.exp(jnp.maximum(diff, 0.0)[None] * log_g[:, None, None]), 0.0).astype(q.dtype)
    q_dec = jnp.exp((i[:, None] + 1.0) * log_g[None, :]).astype(q.dtype)
    k_dec = jnp.exp((c - 1.0 - i)[:, None] * log_g[None, :]).astype(q.dtype)
    c_dec = jnp.exp(c * log_g).astype(q.dtype)

    def split(a):
        return jnp.moveaxis(a.reshape(b, n, c, *a.shape[2:]), 1, 0)

    def step(s, blk):
        qc, kc, vc = blk
        att = jnp.einsum('bihd,bjhd->bhij', qc, kc) * intra
        o = jnp.einsum('bhij,bjhv->bihv', att, vc) + jnp.einsum('bihd,bhdv->bihv', qc * q_dec[None, :, :, None], s)
        s = c_dec[None, :, None, None] * s + jnp.einsum('bjhd,bjhv->bhdv', kc * k_dec[None, :, :, None], vc)
        return s, o

    s, o = lax.scan(step, s0.astype(q.dtype), (split(q), split(k), split(v)))
    return jnp.moveaxis(o, 0, 1).reshape(b, l, RET_HEADS, RET_DV), s


def _complex_affine_combine(e1, e2):
    a1r, a1i, b1r, b1i = e1
    a2r, a2i, b2r, b2i = e2
    return (a2r * a1r - a2i * a1i,
            a2r * a1i + a2i * a1r,
            a2r * b1r - a2i * b1i + b2r,
            a2r * b1i + a2i * b1r + b2i)


def _s5(u, h0_re, h0_im, lam_re, lam_im, log_dt, b_re, b_im, c_re, c_im, d_skip):
    f32 = jnp.float32
    bsz, l = u.shape[:2]
    uf = u.astype(f32)
    ug = uf.reshape(bsz, l, S5_GROUPS, S5_GROUP)
    lr, li = lam_re.astype(f32), lam_im.astype(f32)
    dt = jnp.exp(log_dt.astype(f32))[:, None]
    mag = jnp.exp(lr * dt)
    ab_re, ab_im = mag * jnp.cos(li * dt), mag * jnp.sin(li * dt)
    nr, ni = ab_re - 1.0, ab_im
    den = lr * lr + li * li
    f_re = (nr * lr + ni * li) / den
    f_im = (ni * lr - nr * li) / den
    br, bi = b_re.astype(f32), b_im.astype(f32)
    bb_re = f_re[..., None] * br - f_im[..., None] * bi
    bb_im = f_re[..., None] * bi + f_im[..., None] * br
    bu_re = jnp.einsum('blgc,gpc->lbgp', ug, bb_re)
    bu_im = jnp.einsum('blgc,gpc->lbgp', ug, bb_im)
    h0r, h0i = h0_re.astype(f32), h0_im.astype(f32)
    bu_re = bu_re.at[0].add(ab_re * h0r - ab_im * h0i)
    bu_im = bu_im.at[0].add(ab_re * h0i + ab_im * h0r)
    a_re = jnp.broadcast_to(ab_re, (l, 1) + ab_re.shape)
    a_im = jnp.broadcast_to(ab_im, (l, 1) + ab_im.shape)
    _, _, h_re, h_im = lax.associative_scan(_complex_affine_combine, (a_re, a_im, bu_re, bu_im), axis=0)
    y = (jnp.einsum('lbgp,gcp->blgc', h_re, c_re.astype(f32))
         - jnp.einsum('lbgp,gcp->blgc', h_im, c_im.astype(f32)))
    y = y.reshape(bsz, l, S5_CH) + d_skip.astype(f32) * uf
    return y.astype(u.dtype), h_re[-1], h_im[-1]


def _mixer_ab(xn, pos, past_pos, past_rows, s0, p):
    b, l, _ = xn.shape
    sizes = [MLA_Q_RANK, MLA_KV_RANK, MLA_ROPE, RET_HEADS * RET_DK, RET_HEADS * RET_DK, RET_W]
    c_q, c_kv, k_pe, q_r, k_r, v_r, g_r = jnp.split(xn @ p['w_in'], np.cumsum(sizes).tolist(), axis=-1)
    q = (_rmsnorm(c_q, p['q_norm']) @ p['w_uq']).reshape(b, l, MLA_HEADS, MLA_NOPE + MLA_ROPE)
    q_pe = _rope(q[..., MLA_NOPE:], pos)
    q_lat = jnp.einsum('bqhn,rhn->bqhr', q[..., :MLA_NOPE], p['w_uk'])
    q_full = jnp.concatenate([q_lat, q_pe], axis=-1)
    c_kv = _rmsnorm(c_kv, p['kv_norm'])
    k_pe = _rope(k_pe[:, :, None, :], pos)[:, :, 0, :]
    new_rows = jnp.concatenate([c_kv, k_pe], axis=-1)
    if past_rows is None:
        segs, k_pos = (new_rows,), pos
    else:
        segs, k_pos = (past_rows.astype(new_rows.dtype), new_rows), jnp.concatenate([past_pos, pos])
    o_lat = _mla_attend(q_full, segs, pos, k_pos)[..., :MLA_KV_RANK]
    o_mla = jnp.einsum('bqhr,rhv->bqhv', o_lat, p['w_uv']).reshape(b, l, MLA_W)
    q_r = _rope(q_r.reshape(b, l, RET_HEADS, RET_DK), pos)
    k_r = _rope(k_r.reshape(b, l, RET_HEADS, RET_DK), pos) * (RET_DK ** -0.5)
    v_r = v_r.reshape(b, l, RET_HEADS, RET_DV)
    o_r, s_new = _retention(q_r, k_r, v_r, s0)
    o_r = _headnorm(o_r).reshape(b, l, RET_W) * jax.nn.silu(g_r)
    out = jnp.concatenate([o_mla, o_r], axis=-1) @ p['w_out']
    return out, new_rows, s_new


def _mixer_cd(xn, pos, past_pos, past_k, past_v, h0_re, h0_im, p):
    b, l, _ = xn.shape
    u, q, k, v = jnp.split(xn @ p['w_in'], [S5_CH, S5_CH + SB_W, S5_CH + 2 * SB_W], axis=-1)
    y, h_re, h_im = _s5(u, h0_re, h0_im, p['lam_re'], p['lam_im'], p['log_dt'],
                        p['b_re'], p['b_im'], p['c_re'], p['c_im'], p['d'])
    y = jax.nn.gelu(y)
    o_s5 = y * jax.nn.sigmoid(y @ p['w_glu'])
    q = q.reshape(b, l, SB_HEADS, SB_DIM)
    k = k.reshape(b, l, SB_HEADS, SB_DIM)
    v = v.reshape(b, l, SB_HEADS, SB_DIM)
    if past_k is None:
        k_segs, v_segs, k_pos = (k,), (v,), pos
    else:
        k_segs = (past_k.astype(k.dtype), k)
        v_segs = (past_v.astype(v.dtype), v)
        k_pos = jnp.concatenate([past_pos, pos])
    o_sb = _sb_attend(q, k_segs, v_segs, pos, k_pos).reshape(b, l, SB_W)
    out = jnp.concatenate([o_s5, o_sb], axis=-1) @ p['w_out']
    return out, k, v, h_re, h_im


def _conv_ffn(xn, conv_state, w_up, conv_w, conv_b, w_down):
    l = xn.shape[1]
    h = xn @ w_up
    hp = jnp.concatenate([conv_state.astype(h.dtype), h], axis=1)
    hc = conv_b + conv_w[0] * hp[:, 0:l]
    for j in range(1, CONV_W):
        hc = hc + conv_w[j] * hp[:, j:j + l]
    gate, up = jnp.split(hc, 2, axis=-1)
    out = (jax.nn.gelu(gate, approximate=True) * up) @ w_down
    return out, hp[:, hp.shape[1] - (CONV_W - 1):]


def _run(x, pos, past_pos, past, mix, ffn, norm_gains):
    new = {}
    conv_new = []
    for layer in range(DEPTH):
        g = norm_gains[layer]
        xn = _rmsnorm(x, g[0])
        if layer % 2 == 0:
            m, new['mla'], new['ret'] = _mixer_ab(xn, pos, past_pos, past['mla'], past['ret'], mix[layer])
        else:
            m, new['sb_k'], new['sb_v'], new['s5_re'], new['s5_im'] = _mixer_cd(
                xn, pos, past_pos, past['sb_k'], past['sb_v'], past['s5_re'], past['s5_im'], mix[layer])
        x = x + _rmsnorm(m, g[1])
        f, cs = _conv_ffn(_rmsnorm(x, g[2]), past['ffn_conv'][layer], ffn['w_up'][layer],
                          ffn['conv_w'][layer], ffn['conv_b'][layer], ffn['w_down'][layer])
        x = x + _rmsnorm(f, g[3])
        conv_new.append(cs)
    new['ffn_conv'] = jnp.stack(conv_new)
    return x, new


def setup_inputs(seed: int = 0) -> dict:
    key = jax.random.key(seed)
    ks = iter(jax.random.split(key, 48))

    def nrm(shape, scale):
        return jax.random.normal(next(ks), shape, jnp.float32) * scale

    n_pages = PAST_LEN // PAGE_SIZE
    n_pool = (DEC_BATCH * n_pages * 5) // 4
    inp = {}
    inp['x_prompt'] = nrm((BATCH, SEQ, D_MODEL), 1.0)
    inp['x_sample'] = nrm((DEC_BATCH, DEC_SEQ, D_MODEL), 1.0)
    inp['cache_mla'] = nrm((n_pool, PAGE_SIZE, MLA_ROW), 1.0)
    inp['state_ret'] = nrm((DEC_BATCH, RET_HEADS, RET_DK, RET_DV), 0.5)
    inp['state_s5_re'] = nrm((DEC_BATCH, S5_GROUPS, S5_STATE), 0.1)
    inp['state_s5_im'] = nrm((DEC_BATCH, S5_GROUPS, S5_STATE), 0.1)
    inp['cache_sb_k'] = nrm((n_pool, PAGE_SIZE, SB_HEADS, SB_DIM), 1.0)
    inp['cache_sb_v'] = nrm((n_pool, PAGE_SIZE, SB_HEADS, SB_DIM), 1.0)
    inp['state_ffn_conv'] = nrm((DEPTH, DEC_BATCH, CONV_W - 1, 2 * D_FF), 0.5)
    inp['page_table'] = jax.random.permutation(next(ks), n_pool)[:DEC_BATCH * n_pages].reshape(
        DEC_BATCH, n_pages).astype(jnp.int32)
    inp['norm_gains'] = 1.0 + nrm((DEPTH, 4, D_MODEL), 0.05)
    inp['w_in_l0'] = nrm((D_MODEL, IN0), D_MODEL ** -0.5)
    inp['mla_q_norm'] = 1.0 + nrm((MLA_Q_RANK,), 0.05)
    inp['mla_w_uq'] = nrm((MLA_Q_RANK, MLA_HEADS * (MLA_NOPE + MLA_ROPE)), MLA_Q_RANK ** -0.5)
    inp['mla_kv_norm'] = 1.0 + nrm((MLA_KV_RANK,), 0.05)
    inp['mla_w_uk'] = nrm((MLA_KV_RANK, MLA_HEADS, MLA_NOPE), MLA_KV_RANK ** -0.5)
    inp['mla_w_uv'] = nrm((MLA_KV_RANK, MLA_HEADS, MLA_V), MLA_KV_RANK ** -0.5)
    inp['w_out_l0'] = nrm((MLA_W + RET_W, D_MODEL), (MLA_W + RET_W) ** -0.5)
    inp['w_in_l1'] = nrm((D_MODEL, IN1), D_MODEL ** -0.5)
    inp['s5_lam_re'] = -0.5 + nrm((S5_GROUPS, S5_STATE), 0.01)
    inp['s5_lam_im'] = math.pi * jnp.arange(S5_STATE, dtype=jnp.float32)[None, :] + nrm((S5_GROUPS, S5_STATE), 0.01)
    inp['s5_log_dt'] = jax.random.uniform(next(ks), (S5_GROUPS,), jnp.float32,
                                          minval=math.log(S5_DT_MIN), maxval=math.log(S5_DT_MAX))
    inp['s5_b_re'] = nrm((S5_GROUPS, S5_STATE, S5_GROUP), (2 * S5_GROUP) ** -0.5)
    inp['s5_b_im'] = nrm((S5_GROUPS, S5_STATE, S5_GROUP), (2 * S5_GROUP) ** -0.5)
    inp['s5_c_re'] = nrm((S5_GROUPS, S5_GROUP, S5_STATE), S5_STATE ** -0.5)
    inp['s5_c_im'] = nrm((S5_GROUPS, S5_GROUP, S5_STATE), S5_STATE ** -0.5)
    inp['s5_d'] = nrm((S5_CH,), 0.5)
    inp['s5_w_glu'] = nrm((S5_CH, S5_CH), S5_CH ** -0.5)
    inp['w_out_l1'] = nrm((S5_CH + SB_W, D_MODEL), (S5_CH + SB_W) ** -0.5)
    inp['ffn_w_up'] = nrm((DEPTH, D_MODEL, 2 * D_FF), D_MODEL ** -0.5)
    inp['ffn_conv_w'] = nrm((DEPTH, CONV_W, 2 * D_FF), 0.5)
    inp['ffn_conv_b'] = nrm((DEPTH, 2 * D_FF), 0.02)
    inp['ffn_w_down'] = nrm((DEPTH, D_FF, D_MODEL), D_FF ** -0.5)
    return inp


def reference(x_prompt, x_sample, cache_mla, state_ret, state_s5_re, state_s5_im, cache_sb_k, cache_sb_v,
              state_ffn_conv, page_table, norm_gains, w_in_l0, mla_q_norm, mla_w_uq, mla_kv_norm, mla_w_uk,
              mla_w_uv, w_out_l0, w_in_l1, s5_lam_re, s5_lam_im, s5_log_dt, s5_b_re, s5_b_im, s5_c_re, s5_c_im,
              s5_d, s5_w_glu, w_out_l1, ffn_w_up, ffn_conv_w, ffn_conv_b, ffn_w_down):
    mix = [
        dict(w_in=w_in_l0, q_norm=mla_q_norm, w_uq=mla_w_uq, kv_norm=mla_kv_norm,
             w_uk=mla_w_uk, w_uv=mla_w_uv, w_out=w_out_l0),
        dict(w_in=w_in_l1, lam_re=s5_lam_re, lam_im=s5_lam_im, log_dt=s5_log_dt, b_re=s5_b_re, b_im=s5_b_im,
             c_re=s5_c_re, c_im=s5_c_im, d=s5_d, w_glu=s5_w_glu, w_out=w_out_l1),
    ]
    ffn = dict(w_up=ffn_w_up, conv_w=ffn_conv_w, conv_b=ffn_conv_b, w_down=ffn_w_down)

    bp, lp = x_prompt.shape[0], x_prompt.shape[1]
    dt = x_prompt.dtype
    past_p = dict(
        mla=None, sb_k=None, sb_v=None,
        ret=jnp.zeros((bp, RET_HEADS, RET_DK, RET_DV), dt),
        s5_re=jnp.zeros((bp, S5_GROUPS, S5_STATE), jnp.float32),
        s5_im=jnp.zeros((bp, S5_GROUPS, S5_STATE), jnp.float32),
        ffn_conv=jnp.zeros((DEPTH, bp, CONV_W - 1, 2 * D_FF), dt))
    pos_p = jnp.arange(lp, dtype=jnp.int32)
    y_p, new_p = _run(x_prompt, pos_p, None, past_p, mix, ffn, norm_gains)

    bs, ls = x_sample.shape[0], x_sample.shape[1]
    n_pages = PAST_LEN // PAGE_SIZE

    def gather(cache):
        return cache[page_table].reshape(bs, n_pages * PAGE_SIZE, *cache.shape[2:])

    past_s = dict(mla=gather(cache_mla), sb_k=gather(cache_sb_k), sb_v=gather(cache_sb_v),
                  ret=state_ret, s5_re=state_s5_re, s5_im=state_s5_im, ffn_conv=state_ffn_conv)
    pos_s = PAST_LEN + jnp.arange(ls, dtype=jnp.int32)
    past_pos_s = jnp.arange(n_pages * PAGE_SIZE, dtype=jnp.int32)
    y_s, new_s = _run(x_sample, pos_s, past_pos_s, past_s, mix, ffn, norm_gains)

    return (y_p, y_s,
            new_p['mla'], new_s['mla'],
            new_p['ret'], new_s['ret'],
            new_p['s5_re'], new_p['s5_im'], new_s['s5_re'], new_s['s5_im'],
            new_p['sb_k'], new_p['sb_v'], new_s['sb_k'], new_s['sb_v'],
            new_p['ffn_conv'], new_s['ffn_conv'])
```

```python
import functools
import math

import jax
import jax.numpy as jnp
import numpy as np
from jax import lax
from jax.experimental import pallas as pl
from jax.experimental.pallas import tpu as pltpu

F32 = jnp.float32
BF16 = jnp.bfloat16

D_MODEL = 1024
PAGE_SIZE = 128
MLA_HEADS = 8
MLA_Q_RANK = 384
MLA_KV_RANK = 256
MLA_NOPE = 64
MLA_ROPE = 32
MLA_V = 64
MLA_ROW = MLA_KV_RANK + MLA_ROPE
MLA_W = MLA_HEADS * MLA_V
RET_HEADS = 4
RET_DK = 64
RET_DV = 128
RET_W = RET_HEADS * RET_DV
RET_QK = RET_HEADS * RET_DK
RET_CHUNK = 128
S5_GROUP = 16
S5_CH = 640
S5_GROUPS = S5_CH // S5_GROUP
S5_STATE = 64
S5_HID = S5_GROUPS * S5_STATE
SB_HEADS = 6
SB_DIM = 64
SB_W = SB_HEADS * SB_DIM
D_FF = 2816
CONV_W = 3
ROPE_THETA = 10000.0
EPS = 1e-6
NEG = -1e30

VMEM_LIMIT_BYTES = 56 * 1024 * 1024
LANES = 128


def _cparams(sem):
    return pltpu.CompilerParams(dimension_semantics=sem, vmem_limit_bytes=VMEM_LIMIT_BYTES)


def _const_spec(shape):
    nd = len(shape)
    return pl.BlockSpec(shape, lambda *_: (0,) * nd, pipeline_mode=pl.Buffered(1))


def _rms(x):
    return x * lax.rsqrt(jnp.mean(x * x, axis=-1, keepdims=True) + EPS)


def _gelu_tanh(x):
    c = math.sqrt(2.0 / math.pi)
    return x * (0.5 * (1.0 + jnp.tanh(c * (x + 0.044715 * (x * x * x)))))


def _dot(a, b):
    return jnp.dot(a, b, preferred_element_type=F32)


def _dot_nt(a, b):
    return lax.dot_general(a, b, (((1,), (1,)), ((), ())), preferred_element_type=F32)


def _lane_band(x, lo, width):
    lane = lax.broadcasted_iota(jnp.int32, x.shape, x.ndim - 1)
    return jnp.where((lane >= lo) & (lane < lo + width), x, jnp.zeros_like(x))


_O_CQ, _O_CKV, _O_KPE, _O_KPES = 0, 384, 640, 896
_O_QR, _O_QRS, _O_KR, _O_KRS, _O_VR, _O_GR = 1152, 1408, 1664, 1920, 2176, 2688
IN0_AUG = 3200
_MLA_SCALE = (MLA_NOPE + MLA_ROPE) ** -0.5


def _inproj0_kernel(x_ref, g_ref, win_ref, qn_ref, wuq_ref, kvn_ref, wuk_ref, cm_ref, sm_ref, cr_ref, sr_ref,
                    qlat_ref, qpe_ref, kfull_ref, rows_ref, qr_ref, kr_ref, vr_ref, sg_ref):
    xn = (_rms(x_ref[...]) * g_ref[...]).astype(BF16)
    z = _dot(xn, win_ref[...])
    cqn = (_rms(z[:, _O_CQ:_O_CQ + MLA_Q_RANK]) * qn_ref[...]).astype(BF16)
    q = _dot(cqn, wuq_ref[...])
    for p in range(MLA_HEADS // 2):
        ql = _dot(q[:, 128 * p:128 * (p + 1)].astype(BF16), wuk_ref[p]) * _MLA_SCALE
        qlat_ref[2 * p] = ql[:, :MLA_KV_RANK].astype(qlat_ref.dtype)
        qlat_ref[2 * p + 1] = ql[:, MLA_KV_RANK:].astype(qlat_ref.dtype)
    cm, sm = cm_ref[...], sm_ref[...]
    qpe = (q[:, 512:768] * cm + q[:, 768:1024] * sm) * _MLA_SCALE
    qpe_ref[...] = qpe.astype(qpe_ref.dtype)
    ckv = _rms(z[:, _O_CKV:_O_CKV + MLA_KV_RANK]) * kvn_ref[...]
    kpe = z[:, _O_KPE:_O_KPE + 256] * cm + z[:, _O_KPES:_O_KPES + 256] * sm
    kfull_ref[:, 0:256] = ckv.astype(kfull_ref.dtype)
    kfull_ref[:, 256:512] = kpe.astype(kfull_ref.dtype)
    rows_ref[:, 0:MLA_KV_RANK] = ckv
    rows_ref[:, MLA_KV_RANK:MLA_ROW] = kpe[:, 0:MLA_ROPE]
    cr, sr = cr_ref[...], sr_ref[...]
    qr_ref[...] = z[:, _O_QR:_O_QR + RET_QK] * cr + z[:, _O_QRS:_O_QRS + RET_QK] * sr
    kr_ref[...] = (z[:, _O_KR:_O_KR + RET_QK] * cr + z[:, _O_KRS:_O_KRS + RET_QK] * sr) * (RET_DK ** -0.5)
    vr_ref[...] = z[:, _O_VR:_O_VR + RET_W].astype(vr_ref.dtype)
    g = z[:, _O_GR:_O_GR + RET_W]
    sg_ref[...] = g * jax.nn.sigmoid(g)


def _inproj0(x2d, gain, w, tabs, *, nseq, seqlen, tm, idt):
    nt = seqlen // tm
    cm, sm, cr, sr = tabs
    row_in = lambda b, i: (b * nt + i, 0)
    tm_out = lambda b, i: (i, b)
    tab = lambda b, i: (i, 0)
    out_shape = (
        jax.ShapeDtypeStruct((MLA_HEADS, seqlen, nseq * 256), idt),
        jax.ShapeDtypeStruct((seqlen, nseq * 256), idt),
        jax.ShapeDtypeStruct((seqlen, nseq * 512), idt),
        jax.ShapeDtypeStruct((nseq * seqlen, MLA_ROW), F32),
        jax.ShapeDtypeStruct((seqlen, nseq * RET_QK), F32),
        jax.ShapeDtypeStruct((seqlen, nseq * RET_QK), F32),
        jax.ShapeDtypeStruct((seqlen, nseq * RET_W), idt),
        jax.ShapeDtypeStruct((seqlen, nseq * RET_W), F32),
    )
    out_specs = (
        pl.BlockSpec((MLA_HEADS, tm, 256), lambda b, i: (0, i, b)),
        pl.BlockSpec((tm, 256), tm_out),
        pl.BlockSpec((tm, 512), tm_out),
        pl.BlockSpec((tm, MLA_ROW), row_in),
        pl.BlockSpec((tm, RET_QK), tm_out),
        pl.BlockSpec((tm, RET_QK), tm_out),
        pl.BlockSpec((tm, RET_W), tm_out),
        pl.BlockSpec((tm, RET_W), tm_out),
    )
    in_specs = [
        pl.BlockSpec((tm, D_MODEL), row_in),
        _const_spec((1, D_MODEL)),
        _const_spec((D_MODEL, IN0_AUG)),
        _const_spec((1, MLA_Q_RANK)),
        _const_spec((MLA_Q_RANK, 1024)),
        _const_spec((1, MLA_KV_RANK)),
        _const_spec((MLA_HEADS // 2, 128, 512)),
        pl.BlockSpec((tm, 256), tab), pl.BlockSpec((tm, 256), tab),
        pl.BlockSpec((tm, 256), tab), pl.BlockSpec((tm, 256), tab),
    ]
    return pl.pallas_call(
        _inproj0_kernel, grid=(nseq, nt), in_specs=in_specs, out_specs=out_specs, out_shape=out_shape,
        compiler_params=_cparams(("parallel", "parallel")), name="inproj0",
    )(x2d, gain, w["w_in0"], w["q_norm"], w["w_uq"], w["kv_norm"], w["w_uk"], cm, sm, cr, sr)


def _softmax_step(s, v, m_ref, l_ref, acc_ref):
    m_prev = m_ref[...]
    m_new = jnp.maximum(m_prev, jnp.max(s, axis=-1, keepdims=True))
    alpha = jnp.exp(m_prev - m_new)
    p = jnp.exp(s - m_new)
    l_ref[...] = alpha * l_ref[...] + jnp.sum(p, axis=-1, keepdims=True)
    acc_ref[...] = alpha * acc_ref[...] + _dot(p.astype(BF16), v)
    m_ref[...] = m_new


def _mla_prompt_kernel(qlat_ref, qpe_ref, k_ref, wuv_ref, o_ref, qs_ref, m_ref, l_ref, acc_ref, *, tq):
    qi = pl.program_id(1)
    rows = MLA_HEADS * tq
    qpe = qpe_ref[...]
    for h in range(MLA_HEADS):
        qs_ref[h * tq:(h + 1) * tq, 0:256] = qlat_ref[h]
        qs_ref[h * tq:(h + 1) * tq, 256:512] = _lane_band(qpe, MLA_ROPE * h, MLA_ROPE)
    m_ref[...] = jnp.full((rows, 1), NEG, F32)
    l_ref[...] = jnp.zeros((rows, 1), F32)
    acc_ref[...] = jnp.zeros((rows, MLA_KV_RANK), F32)

    def scores(j):
        k = k_ref[pl.ds(pl.multiple_of(j * tq, tq), tq), :]
        return _dot_nt(qs_ref[...], k), k[:, 0:MLA_KV_RANK]

    def body(j, c):
        s, v = scores(j)
        _softmax_step(s, v, m_ref, l_ref, acc_ref)
        return c

    lax.fori_loop(0, qi, body, 0)
    s, v = scores(qi)
    t_q = lax.broadcasted_iota(jnp.int32, (rows, tq), 0) % tq
    t_k = lax.broadcasted_iota(jnp.int32, (rows, tq), 1)
    _softmax_step(jnp.where(t_k <= t_q, s, NEG), v, m_ref, l_ref, acc_ref)
    o_lat = (acc_ref[...] / l_ref[...]).astype(BF16)
    out = _dot(o_lat[0:tq], wuv_ref[0])
    for h in range(1, MLA_HEADS):
        out = out + _dot(o_lat[h * tq:(h + 1) * tq], wuv_ref[h])
    o_ref[...] = out.astype(o_ref.dtype)


def _mla_prompt(qlat, qpe, kfull, wuv, *, nseq, seqlen, tq):
    nq = seqlen // tq
    rows = MLA_HEADS * tq
    return pl.pallas_call(
        functools.partial(_mla_prompt_kernel, tq=tq),
        grid=(nseq, nq),
        in_specs=[
            pl.BlockSpec((MLA_HEADS, tq, 256), lambda b, i: (0, i, b)),
            pl.BlockSpec((tq, 256), lambda b, i: (i, b)),
            pl.BlockSpec((seqlen, 512), lambda b, i: (0, b)),
            _const_spec((MLA_HEADS, MLA_KV_RANK, MLA_W)),
        ],
        out_specs=pl.BlockSpec((tq, MLA_W), lambda b, i: (i, b)),
        out_shape=jax.ShapeDtypeStruct((seqlen, nseq * MLA_W), BF16),
        scratch_shapes=[pltpu.VMEM((rows, 512), BF16), pltpu.VMEM((rows, 1), F32), pltpu.VMEM((rows, 1), F32),
                        pltpu.VMEM((rows, MLA_KV_RANK), F32)],
        compiler_params=_cparams(("parallel", "parallel")), name="mla_prompt",
    )(qlat, qpe, kfull, wuv)


def _mla_sample_kernel(pt_ref, qs_ref, knew_ref, wuv_ref, *refs, npage, nstep, lnew):
    pages = refs[:npage]
    o_ref = refs[npage]
    m_ref, l_ref, acc_ref, kpad_ref = refs[npage + 1:]
    step = pl.program_id(1)
    rows = MLA_HEADS * lnew

    @pl.when(step == 0)
    def _():
        m_ref[...] = jnp.full((rows, 1), NEG, F32)
        l_ref[...] = jnp.zeros((rows, 1), F32)
        acc_ref[...] = jnp.zeros((rows, MLA_KV_RANK), F32)

    q = qs_ref[0].astype(BF16)
    kps = [pages[n][0].astype(BF16) for n in range(npage)]
    s = jnp.concatenate([_dot_nt(q, kp) for kp in kps], axis=1)
    m_prev = m_ref[...]
    m_new = jnp.maximum(m_prev, jnp.max(s, axis=-1, keepdims=True))
    alpha = jnp.exp(m_prev - m_new)
    p = jnp.exp(s - m_new)
    l_ref[...] = alpha * l_ref[...] + jnp.sum(p, axis=-1, keepdims=True)
    pv = _dot(p[:, 0:PAGE_SIZE].astype(BF16), kps[0][:, 0:MLA_KV_RANK])
    for n in range(1, npage):
        pv = pv + _dot(p[:, n * PAGE_SIZE:(n + 1) * PAGE_SIZE].astype(BF16), kps[n][:, 0:MLA_KV_RANK])
    acc_ref[...] = alpha * acc_ref[...] + pv
    m_ref[...] = m_new

    @pl.when(step == nstep - 1)
    def _():
        kpad_ref[...] = jnp.zeros((PAGE_SIZE, MLA_ROW), F32)
        kpad_ref[0:lnew, :] = knew_ref[0]
        kn = kpad_ref[...].astype(BF16)
        sn = _dot_nt(q, kn)
        t_q = lax.broadcasted_iota(jnp.int32, (rows, PAGE_SIZE), 0) % lnew
        t_k = lax.broadcasted_iota(jnp.int32, (rows, PAGE_SIZE), 1)
        _softmax_step(jnp.where(t_k <= t_q, sn, NEG), kn[:, 0:MLA_KV_RANK], m_ref, l_ref, acc_ref)
        o_lat = (acc_ref[...] / l_ref[...]).astype(BF16)
        out = _dot(o_lat[0:lnew], wuv_ref[0])
        for h in range(1, MLA_HEADS):
            out = out + _dot(o_lat[h * lnew:(h + 1) * lnew], wuv_ref[h])
        o_ref[0] = out


def _mla_sample(page_table, qs, knew, wuv, cache, *, npage):
    nseq, n_pages = page_table.shape
    lnew = knew.shape[1]
    rows = MLA_HEADS * lnew
    nstep = n_pages // npage
    page_specs = [
        pl.BlockSpec((1, PAGE_SIZE, MLA_ROW), lambda b, s, pt, n=n: (pt[b, s * npage + n], 0, 0))
        for n in range(npage)
    ]
    grid_spec = pltpu.PrefetchScalarGridSpec(
        num_scalar_prefetch=1, grid=(nseq, nstep),
        in_specs=[
            pl.BlockSpec((1, rows, MLA_ROW), lambda b, s, pt: (b, 0, 0)),
            pl.BlockSpec((1, lnew, MLA_ROW), lambda b, s, pt: (b, 0, 0)),
            pl.BlockSpec((MLA_HEADS, MLA_KV_RANK, MLA_W), lambda b, s, pt: (0, 0, 0)),
        ] + page_specs,
        out_specs=pl.BlockSpec((1, lnew, MLA_W), lambda b, s, pt: (b, 0, 0)),
        scratch_shapes=[pltpu.VMEM((rows, 1), F32), pltpu.VMEM((rows, 1), F32),
                        pltpu.VMEM((rows, MLA_KV_RANK), F32), pltpu.VMEM((PAGE_SIZE, MLA_ROW), F32)],
    )
    return pl.pallas_call(
        functools.partial(_mla_sample_kernel, npage=npage, nstep=nstep, lnew=lnew),
        grid_spec=grid_spec,
        out_shape=jax.ShapeDtypeStruct((nseq, lnew, MLA_W), F32),
        compiler_params=_cparams(("parallel", "arbitrary")), name="mla_sample",
    )(page_table, qs, knew, wuv, *([cache] * npage))


def _retention_kernel(q_ref, k_ref, v_ref, sg_ref, intra_ref, qdec_ref, kdec_ref, cdec_ref, s0_ref,
                      o_ref, so_ref, s_ref, pad_ref, *, c, nchunk, lrows):
    i = pl.program_id(1)

    @pl.when(i == 0)
    def _():
        s_ref[...] = s0_ref[0]

    row = lax.broadcasted_iota(jnp.int32, (RET_QK, RET_DV), 0)
    for ch in range(nchunk):
        if lrows < c:
            pad_ref[...] = jnp.zeros(pad_ref.shape, F32)
            pad_ref[0:lrows, 0:RET_QK] = q_ref[...]
            pad_ref[0:lrows, RET_QK:2 * RET_QK] = k_ref[...]
            pad_ref[0:lrows, 2 * RET_QK:2 * RET_QK + RET_W] = v_ref[...].astype(F32)
            q = pad_ref[:, 0:RET_QK]
            k = pad_ref[:, RET_QK:2 * RET_QK]
            v_all = pad_ref[:, 2 * RET_QK:2 * RET_QK + RET_W].astype(BF16)
        else:
            rs = slice(ch * c, (ch + 1) * c)
            q, k, v_all = q_ref[rs, :], k_ref[rs, :], v_ref[rs, :].astype(BF16)
        qd = q * qdec_ref[...]
        kdt = (k * kdec_ref[...]).T.astype(BF16)
        kb = k.astype(BF16)
        s_old = s_ref[...]
        sb = s_old.astype(BF16)
        s_new = cdec_ref[...] * s_old
        for h in range(RET_HEADS):
            qh = _lane_band(q, RET_DK * h, RET_DK).astype(BF16)
            qdh = _lane_band(qd, RET_DK * h, RET_DK).astype(BF16)
            vh = v_all[:, RET_DV * h:RET_DV * (h + 1)]
            att = _dot_nt(qh, kb) * intra_ref[h]
            o = _dot(att.astype(BF16), vh) + _dot(qdh, sb)
            o = _rms(o)
            if lrows < c:
                o_ref[:, RET_DV * h:RET_DV * (h + 1)] = (
                    o[0:lrows] * sg_ref[:, RET_DV * h:RET_DV * (h + 1)]).astype(o_ref.dtype)
            else:
                o_ref[rs, RET_DV * h:RET_DV * (h + 1)] = (
                    o * sg_ref[rs, RET_DV * h:RET_DV * (h + 1)]).astype(o_ref.dtype)
            upd = _dot(kdt, vh)
            s_new = s_new + jnp.where((row >= RET_DK * h) & (row < RET_DK * (h + 1)), upd, 0.0)
        s_ref[...] = s_new

    @pl.when(i == pl.num_programs(1) - 1)
    def _():
        so_ref[0] = s_ref[...]


def _retention_tables(c_eff, c_pad):
    log_g = jnp.log(1.0 - 2.0 ** (-5.0 - jnp.arange(RET_HEADS, dtype=F32)))
    i = jnp.arange(c_pad, dtype=F32)
    diff = i[:, None] - i[None, :]
    intra = jnp.where(diff >= 0, jnp.exp(jnp.maximum(diff, 0.0)[None] * log_g[:, None, None]), 0.0)
    q_dec = jnp.exp((i[:, None] + 1.0) * log_g[None, :])
    k_dec = jnp.exp(jnp.maximum(c_eff - 1.0 - i, 0.0)[:, None] * log_g[None, :])
    c_dec = jnp.exp(c_eff * log_g)
    rep = lambda a: jnp.repeat(a, RET_DK, axis=1)
    cdec_rows = jnp.broadcast_to(jnp.repeat(c_dec, RET_DK)[:, None], (RET_QK, RET_DV))
    return intra.astype(F32), rep(q_dec), rep(k_dec), cdec_rows


def _retention(qr, kr, vr, sg, s0, *, nseq, seqlen, odt):
    c = RET_CHUNK
    if seqlen % c == 0:
        c_eff, tr = c, min(seqlen, 4 * c)
        nchunk, lrows = tr // c, tr
    else:
        c_eff, tr, nchunk, lrows = seqlen, seqlen, 1, seqlen
    nt = seqlen // tr
    intra, qdec, kdec, cdec = _retention_tables(c_eff, c)
    blk = lambda w: pl.BlockSpec((tr, w), lambda b, i: (i, b))
    return pl.pallas_call(
        functools.partial(_retention_kernel, c=c, nchunk=nchunk, lrows=lrows),
        grid=(nseq, nt),
        in_specs=[blk(RET_QK), blk(RET_QK), blk(RET_W), blk(RET_W),
                  _const_spec((RET_HEADS, c, c)), _const_spec((c, RET_QK)), _const_spec((c, RET_QK)),
                  _const_spec((RET_QK, RET_DV)),
                  pl.BlockSpec((1, RET_QK, RET_DV), lambda b, i: (b, 0, 0))],
        out_specs=(blk(RET_W), pl.BlockSpec((1, RET_QK, RET_DV), lambda b, i: (b, 0, 0))),
        out_shape=(jax.ShapeDtypeStruct((seqlen, nseq * RET_W), odt),
                   jax.ShapeDtypeStruct((nseq, RET_QK, RET_DV), F32)),
        scratch_shapes=[pltpu.VMEM((RET_QK, RET_DV), F32), pltpu.VMEM((c, 2 * RET_QK + RET_W), F32)],
        compiler_params=_cparams(("parallel", "arbitrary")), name="retention",
    )(qr, kr, vr, sg, intra, qdec, kdec, cdec, s0)


def _outproj_kernel(x_ref, a_ref, b_ref, wa_ref, wb_ref, g_ref, y_ref):
    m = _dot(a_ref[...].astype(BF16), wa_ref[...]) + _dot(b_ref[...].astype(BF16), wb_ref[...])
    y_ref[...] = x_ref[...] + _rms(m) * g_ref[...]


def _outproj(x, a, b, wa, wb, gain, *, grid, x_map, ab_map, y_map, y_shape, tm):
    wa_w, wb_w = wa.shape[0], wb.shape[0]
    return pl.pallas_call(
        _outproj_kernel, grid=grid,
        in_specs=[pl.BlockSpec((tm, D_MODEL), x_map), pl.BlockSpec((tm, wa_w), ab_map),
                  pl.BlockSpec((tm, wb_w), ab_map), _const_spec(wa.shape), _const_spec(wb.shape),
                  _const_spec((1, D_MODEL))],
        out_specs=pl.BlockSpec((tm, D_MODEL), y_map),
        out_shape=jax.ShapeDtypeStruct(y_shape, F32),
        compiler_params=_cparams(("parallel",) * len(grid)), name="outproj",
    )(x, a, b, wa, wb, gain)


_FFN_CK = 256


def _ffn_kernel(x_ref, g2_ref, wup_ref, cw_ref, cb_ref, wdn_ref, g3_ref, st_ref, y_ref, cs_ref,
                carry_ref, hp_ref, acc_ref, *, nb, tm):
    i = pl.program_id(0)
    ck = _FFN_CK

    @pl.when(i == 0)
    def _():
        carry_ref[...] = st_ref[...]

    x = x_ref[...]
    xn = (_rms(x) * g2_ref[...]).astype(BF16)
    acc_ref[...] = jnp.zeros((tm, D_MODEL), F32)
    for c in range(D_FF // ck):
        act = []
        for half in range(2):
            col = half * D_FF + c * ck
            cs = slice(col, col + ck)
            h = _dot(xn, wup_ref[:, cs])
            hp_ref[0:2 * nb, :] = carry_ref[:, cs]
            hp_ref[2 * nb:2 * nb + tm, :] = h
            carry_ref[:, cs] = hp_ref[tm:tm + 2 * nb, :]
            act.append(cb_ref[:, cs] + cw_ref[0:1, cs] * hp_ref[0:tm, :] + cw_ref[1:2, cs] * hp_ref[nb:nb + tm, :]
                       + cw_ref[2:3, cs] * h)
        a = (_gelu_tanh(act[0]) * act[1]).astype(BF16)
        acc_ref[...] += _dot(a, wdn_ref[c * ck:(c + 1) * ck, :])
    y_ref[...] = x + _rms(acc_ref[...]) * g3_ref[...]

    @pl.when(i == pl.num_programs(0) - 1)
    def _():
        cs_ref[...] = carry_ref[...]


def _ffn(x, g2, wup, cw, cb, wdn, g3, state, *, nb, tm):
    rows = x.shape[0]
    assert tm >= 2 * nb and tm % nb == 0 and rows % tm == 0
    blk = pl.BlockSpec((tm, D_MODEL), lambda i: (i, 0))
    return pl.pallas_call(
        functools.partial(_ffn_kernel, nb=nb, tm=tm),
        grid=(rows // tm,),
        in_specs=[blk, _const_spec((1, D_MODEL)), _const_spec((D_MODEL, 2 * D_FF)), _const_spec((CONV_W, 2 * D_FF)),
                  _const_spec((1, 2 * D_FF)), _const_spec((D_FF, D_MODEL)), _const_spec((1, D_MODEL)),
                  _const_spec((2 * nb, 2 * D_FF))],
        out_specs=(blk, pl.BlockSpec((2 * nb, 2 * D_FF), lambda i: (0, 0))),
        out_shape=(jax.ShapeDtypeStruct((rows, D_MODEL), F32), jax.ShapeDtypeStruct((2 * nb, 2 * D_FF), F32)),
        scratch_shapes=[pltpu.VMEM((2 * nb, 2 * D_FF), F32), pltpu.VMEM((tm + 2 * nb, _FFN_CK), F32),
                        pltpu.VMEM((tm, D_MODEL), F32)],
        compiler_params=_cparams(("arbitrary",)), name="convffn",
    )(x, g2, wup, cw, cb, wdn, g3, state)


def _inproj1_kernel(x_ref, g_ref, win_ref, u_ref, q_ref, k_ref, v_ref, kb_ref, vb_ref):
    xn = (_rms(x_ref[...]) * g_ref[...]).astype(BF16)
    z = _dot(xn, win_ref[...])
    u_ref[...] = z[:, 0:S5_CH]
    q_ref[...] = (z[:, S5_CH:S5_CH + SB_W] * (SB_DIM ** -0.5)).astype(q_ref.dtype)
    k = z[:, S5_CH + SB_W:S5_CH + 2 * SB_W]
    v = z[:, S5_CH + 2 * SB_W:S5_CH + 3 * SB_W]
    k_ref[...] = k
    v_ref[...] = v
    kb_ref[...] = k.astype(kb_ref.dtype)
    vb_ref[...] = v.astype(vb_ref.dtype)


def _inproj1(x, gain, win, *, tm, idt):
    rows = x.shape[0]
    blk = lambda w: pl.BlockSpec((tm, w), lambda i: (i, 0))
    sds = lambda w, dt: jax.ShapeDtypeStruct((rows, w), dt)
    return pl.pallas_call(
        _inproj1_kernel, grid=(rows // tm,),
        in_specs=[blk(D_MODEL), _const_spec((1, D_MODEL)), _const_spec(win.shape)],
        out_specs=(blk(S5_CH), blk(SB_W), blk(SB_W), blk(SB_W), blk(SB_W), blk(SB_W)),
        out_shape=(sds(S5_CH, F32), sds(SB_W, idt), sds(SB_W, F32), sds(SB_W, F32), sds(SB_W, idt), sds(SB_W, idt)),
        compiler_params=_cparams(("parallel",)), name="inproj1",
    )(x, gain, win)


_S5_LW = 256


def _s5_kernel(u_ref, h0r_ref, h0i_ref, br_ref, bi_ref, ar_ref, ai_ref, cr_ref, cin_ref, d_ref, wg_ref,
               o_ref, hr_ref, hi_ref, sr_ref, si_ref, *, tc, bb):
    i = pl.program_id(1)

    @pl.when(i == 0)
    def _():
        hr_ref[...] = h0r_ref[...]
        hi_ref[...] = h0i_ref[...]

    u = u_ref[...].reshape(tc * bb, S5_CH)
    ub = u.astype(BF16)
    sr_ref[...] = _dot(ub, br_ref[...])
    si_ref[...] = _dot(ub, bi_ref[...])
    for cb in range(S5_HID // _S5_LW):
        cs = slice(cb * _S5_LW, (cb + 1) * _S5_LW)
        ar, ai = ar_ref[:, cs], ai_ref[:, cs]

        def body(t, carry, cs=cs, ar=ar, ai=ai):
            hr, hi = carry
            rs = pl.ds(pl.multiple_of(t * bb, bb), bb)
            nr = ar * hr - ai * hi + sr_ref[rs, cs]
            ni = ar * hi + ai * hr + si_ref[rs, cs]
            sr_ref[rs, cs] = nr
            si_ref[rs, cs] = ni
            return nr, ni

        hr, hi = lax.fori_loop(0, tc, body, (hr_ref[:, cs], hi_ref[:, cs]))
        hr_ref[:, cs] = hr
        hi_ref[:, cs] = hi
    y = _dot(sr_ref[...].astype(BF16), cr_ref[...]) + _dot(si_ref[...].astype(BF16), cin_ref[...])
    y = _gelu_tanh(y + d_ref[...] * u)
    o = y * jax.nn.sigmoid(_dot(y.astype(BF16), wg_ref[...]))
    o_ref[...] = o.reshape(tc, bb, S5_CH).astype(o_ref.dtype)


def _s5(u3, h0r, h0i, w, *, tc, bb, odt):
    tlen, nseq, _ = u3.shape
    ublk = pl.BlockSpec((tc, bb, S5_CH), lambda j, i: (i, j, 0))
    hblk = pl.BlockSpec((bb, S5_HID), lambda j, i: (j, 0))
    return pl.pallas_call(
        functools.partial(_s5_kernel, tc=tc, bb=bb),
        grid=(nseq // bb, tlen // tc),
        in_specs=[ublk, hblk, hblk, _const_spec((S5_CH, S5_HID)), _const_spec((S5_CH, S5_HID)),
                  _const_spec((1, S5_HID)), _const_spec((1, S5_HID)), _const_spec((S5_HID, S5_CH)),
                  _const_spec((S5_HID, S5_CH)), _const_spec((1, S5_CH)), _const_spec((S5_CH, S5_CH))],
        out_specs=(ublk, hblk, hblk),
        out_shape=(jax.ShapeDtypeStruct((tlen, nseq, S5_CH), odt), jax.ShapeDtypeStruct((nseq, S5_HID), F32),
                   jax.ShapeDtypeStruct((nseq, S5_HID), F32)),
        scratch_shapes=[pltpu.VMEM((tc * bb, S5_HID), F32), pltpu.VMEM((tc * bb, S5_HID), F32)],
        compiler_params=_cparams(("parallel", "arbitrary")), name="s5",
    )(u3, h0r, h0i, w["bbr"], w["bbi"], w["abr"], w["abi"], w["ccr"], w["ccin"], w["d"], w["w_glu"])


def _sb_block(z, v, tri, carry, valid):
    t = jnp.log(1.0 + jnp.exp(-jnp.abs(z)))
    lsz = jnp.minimum(z, 0.0) - t
    lk = lsz - z
    if valid is not None:
        lk = jnp.where(valid, lk, 0.0)
    hi = lk.astype(BF16)
    lo = (lk - hi.astype(F32)).astype(BF16)
    later = _dot(hi, tri) + _dot(lo, tri) + carry
    w = jnp.exp(lsz + later)
    if valid is not None:
        w = jnp.where(valid, w, 0.0)
    return _dot(w.astype(BF16), v), carry + jnp.sum(lk, axis=-1, keepdims=True)


def _sb_prompt_kernel(q_ref, kb_ref, vb_ref, k32_ref, v32_ref, tri_ref, o_ref, ko_ref, vo_ref, *, tq):
    qi = pl.program_id(1)
    ko_ref[...] = k32_ref[...]
    vo_ref[...] = v32_ref[...]
    tri = tri_ref[...]
    t_q = lax.broadcasted_iota(jnp.int32, (tq, tq), 0)
    t_k = lax.broadcasted_iota(jnp.int32, (tq, tq), 1)
    diag_valid = t_k < t_q
    lane = lax.broadcasted_iota(jnp.int32, (tq, LANES), 1)
    for p in range(SB_HEADS // 2):
        ps = slice(LANES * p, LANES * (p + 1))
        qp = q_ref[:, ps]
        outs = []
        for hh in range(2):
            qh = _lane_band(qp, SB_DIM * hh, SB_DIM)

            def block(j, carry, acc, valid, qh=qh, ps=ps):
                rs = pl.ds(pl.multiple_of(j * tq, tq), tq)
                z = _dot_nt(qh, kb_ref[rs, ps])
                pv, carry = _sb_block(z, vb_ref[rs, ps], tri, carry, valid)
                return carry, acc + pv

            carry, acc = block(qi, jnp.zeros((tq, 1), F32), jnp.zeros((tq, LANES), F32), diag_valid)
            carry, acc = lax.fori_loop(0, qi, lambda jj, ca: block(qi - 1 - jj, ca[0], ca[1], None), (carry, acc))
            outs.append(acc)
        o_ref[:, ps] = jnp.where(lane < SB_DIM, outs[0], outs[1]).astype(o_ref.dtype)


def _tri(n):
    j = jnp.arange(n)
    return (j[:, None] > j[None, :]).astype(BF16)


def _sb_prompt(q, kb, vb, k32, v32, *, nseq, seqlen, tq):
    nq = seqlen // tq
    colblk = pl.BlockSpec((tq, SB_W), lambda b, i: (i, b))
    seqblk = pl.BlockSpec((seqlen, SB_W), lambda b, i: (0, b))
    rowblk = pl.BlockSpec((tq, SB_W), lambda b, i: (b * nq + i, 0))
    return pl.pallas_call(
        functools.partial(_sb_prompt_kernel, tq=tq),
        grid=(nseq, nq),
        in_specs=[colblk, seqblk, seqblk, colblk, colblk, _const_spec((tq, tq))],
        out_specs=(colblk, rowblk, rowblk),
        out_shape=(jax.ShapeDtypeStruct((seqlen, nseq * SB_W), BF16),
                   jax.ShapeDtypeStruct((nseq * seqlen, SB_W), F32),
                   jax.ShapeDtypeStruct((nseq * seqlen, SB_W), F32)),
        compiler_params=_cparams(("parallel", "parallel")), name="sb_prompt",
    )(q, kb, vb, k32, v32, _tri(tq))


def _sb_sample_kernel(pt_ref, q_ref, knew_ref, vnew_ref, tri_ref, *refs, npage, nstep, lnew):
    kpages = refs[:npage]
    vpages = refs[npage:2 * npage]
    o_ref = refs[2 * npage]
    carry_ref, acc_ref, pad_ref = refs[2 * npage + 1:]
    step = pl.program_id(1)
    rows = SB_HEADS * lnew
    tri = tri_ref[...]
    qs = [q_ref[0, h].astype(BF16) for h in range(SB_HEADS)]

    def visit(k_of, v_of, valid):
        z = jnp.concatenate([_dot_nt(qs[h], k_of(h)) for h in range(SB_HEADS)], axis=0)
        t = jnp.log(1.0 + jnp.exp(-jnp.abs(z)))
        lsz = jnp.minimum(z, 0.0) - t
        lk = lsz - z
        if valid is not None:
            lk = jnp.where(valid, lk, 0.0)
        hi = lk.astype(BF16)
        lo = (lk - hi.astype(F32)).astype(BF16)
        later = _dot(hi, tri) + _dot(lo, tri) + carry_ref[...]
        w = jnp.exp(lsz + later)
        if valid is not None:
            w = jnp.where(valid, w, 0.0)
        wb = w.astype(BF16)
        for h in range(SB_HEADS):
            acc_ref[h] += _dot(wb[h * lnew:(h + 1) * lnew], v_of(h))
        carry_ref[...] += jnp.sum(lk, axis=-1, keepdims=True)

    @pl.when(step == 0)
    def _():
        carry_ref[...] = jnp.zeros((rows, 1), F32)
        acc_ref[...] = jnp.zeros(acc_ref.shape, F32)
        pad_ref[...] = jnp.zeros(pad_ref.shape, F32)
        pad_ref[0, 0:lnew, :] = knew_ref[0]
        pad_ref[1, 0:lnew, :] = vnew_ref[0]
        t_q = lax.broadcasted_iota(jnp.int32, (rows, PAGE_SIZE), 0) % lnew
        t_k = lax.broadcasted_iota(jnp.int32, (rows, PAGE_SIZE), 1)
        visit(lambda h: pad_ref[0, :, SB_DIM * h:SB_DIM * (h + 1)].astype(BF16),
              lambda h: pad_ref[1, :, SB_DIM * h:SB_DIM * (h + 1)].astype(BF16), t_k < t_q)

    for n in range(npage):
        visit(lambda h, n=n: kpages[n][0, :, h, :].astype(BF16),
              lambda h, n=n: vpages[n][0, :, h, :].astype(BF16), None)

    @pl.when(step == nstep - 1)
    def _():
        o_ref[0] = acc_ref[...]


def _sb_sample(page_table, q, knew, vnew, cache_k, cache_v, *, npage):
    nseq, n_pages = page_table.shape
    lnew = knew.shape[1]
    rows = SB_HEADS * lnew
    nstep = n_pages // npage
    pidx = lambda b, s, pt, n: (pt[b, n_pages - 1 - (s * npage + n)], 0, 0, 0)
    kspecs = [pl.BlockSpec((1, PAGE_SIZE, SB_HEADS, SB_DIM), functools.partial(pidx, n=n)) for n in range(npage)]
    vspecs = [pl.BlockSpec((1, PAGE_SIZE, SB_HEADS, SB_DIM), functools.partial(pidx, n=n)) for n in range(npage)]
    grid_spec = pltpu.PrefetchScalarGridSpec(
        num_scalar_prefetch=1, grid=(nseq, nstep),
        in_specs=[
            pl.BlockSpec((1, SB_HEADS, lnew, SB_DIM), lambda b, s, pt: (b, 0, 0, 0)),
            pl.BlockSpec((1, lnew, SB_W), lambda b, s, pt: (b, 0, 0)),
            pl.BlockSpec((1, lnew, SB_W), lambda b, s, pt: (b, 0, 0)),
            pl.BlockSpec((PAGE_SIZE, PAGE_SIZE), lambda b, s, pt: (0, 0)),
        ] + kspecs + vspecs,
        out_specs=pl.BlockSpec((1, SB_HEADS, lnew, SB_DIM), lambda b, s, pt: (b, 0, 0, 0)),
        scratch_shapes=[pltpu.VMEM((rows, 1), F32), pltpu.VMEM((SB_HEADS, lnew, SB_DIM), F32),
                        pltpu.VMEM((2, PAGE_SIZE, SB_W), F32)],
    )
    return pl.pallas_call(
        functools.partial(_sb_sample_kernel, npage=npage, nstep=nstep, lnew=lnew),
        grid_spec=grid_spec,
        out_shape=jax.ShapeDtypeStruct((nseq, SB_HEADS, lnew, SB_DIM), F32),
        compiler_params=_cparams(("parallel", "arbitrary")), name="sb_sample",
    )(page_table, q, knew, vnew, _tri(PAGE_SIZE), *([cache_k] * npage), *([cache_v] * npage))


def _copy_kernel(x_ref, y_ref):
    y_ref[...] = x_ref[...]


def _to_seq_major(x_tm, *, nseq, seqlen, tm):
    nt = seqlen // tm
    return pl.pallas_call(
        _copy_kernel, grid=(nseq, nt),
        in_specs=[pl.BlockSpec((tm, D_MODEL), lambda b, i: (i, b))],
        out_specs=pl.BlockSpec((tm, D_MODEL), lambda b, i: (b * nt + i, 0)),
        out_shape=jax.ShapeDtypeStruct((nseq * seqlen, D_MODEL), F32),
        compiler_params=_cparams(("parallel", "parallel")), name="to_seq_major",
    )(x_tm)


def _swap_halves(n_heads, dim):
    idx = np.arange(n_heads * dim)
    return (idx // dim) * dim + (idx % dim + dim // 2) % dim


def _rope_tables(pos, half, reps):
    freqs = ROPE_THETA ** (-jnp.arange(half, dtype=F32) / half)
    ang = pos.astype(F32)[:, None] * freqs[None, :]
    c, s = jnp.cos(ang), jnp.sin(ang)
    return (jnp.tile(jnp.concatenate([c, c], axis=-1), (1, reps)),
            jnp.tile(jnp.concatenate([-s, s], axis=-1), (1, reps)))


def _prep_layer0(w_in, q_norm, w_uq, kv_norm, w_uk, w_uv, w_out):
    o_kpe = MLA_Q_RANK + MLA_KV_RANK
    o_qr = o_kpe + MLA_ROPE
    o_kr = o_qr + RET_QK
    o_v = o_kr + RET_QK
    cols = np.concatenate([
        np.arange(0, o_kpe),
        np.tile(o_kpe + np.arange(MLA_ROPE), MLA_HEADS), np.tile(o_kpe + _swap_halves(1, MLA_ROPE), MLA_HEADS),
        o_qr + np.arange(RET_QK), o_qr + _swap_halves(RET_HEADS, RET_DK),
        o_kr + np.arange(RET_QK), o_kr + _swap_halves(RET_HEADS, RET_DK),
        np.arange(o_v, o_v + 2 * RET_W)])
    assert cols.shape[0] == IN0_AUG
    hd = MLA_NOPE + MLA_ROPE
    heads = np.arange(MLA_HEADS)[:, None] * hd
    uq_cols = np.concatenate([
        (heads + np.arange(MLA_NOPE)[None, :]).ravel(),
        (heads + MLA_NOPE + np.arange(MLA_ROPE)[None, :]).ravel(),
        (heads + MLA_NOPE + _swap_halves(1, MLA_ROPE)[None, :]).ravel()])
    wuk_t = jnp.transpose(w_uk, (1, 2, 0))
    z = jnp.zeros((MLA_NOPE, MLA_KV_RANK), F32)
    wuk_pair = jnp.stack([jnp.block([[wuk_t[2 * p], z], [z, wuk_t[2 * p + 1]]]) for p in range(MLA_HEADS // 2)])
    wuv_pad = jnp.zeros((MLA_HEADS, MLA_KV_RANK, MLA_W), F32)
    for h in range(MLA_HEADS):
        wuv_pad = wuv_pad.at[h, :, MLA_V * h:MLA_V * (h + 1)].set(w_uv[:, h, :])
    return dict(w_in0=w_in[:, cols].astype(BF16), q_norm=q_norm[None, :], w_uq=w_uq[:, uq_cols].astype(BF16),
                kv_norm=kv_norm[None, :], w_uk=wuk_pair.astype(BF16), w_uv=wuv_pad.astype(BF16),
                w_out_a=w_out[:MLA_W].astype(BF16), w_out_b=w_out[MLA_W:].astype(BF16))


def _prep_s5(lam_re, lam_im, log_dt, b_re, b_im, c_re, c_im, d_skip, w_glu):
    lr, li = lam_re.astype(F32), lam_im.astype(F32)
    dt = jnp.exp(log_dt.astype(F32))[:, None]
    mag = jnp.exp(lr * dt)
    ab_re, ab_im = mag * jnp.cos(li * dt), mag * jnp.sin(li * dt)
    nr, ni = ab_re - 1.0, ab_im
    den = lr * lr + li * li
    f_re = (nr * lr + ni * li) / den
    f_im = (ni * lr - nr * li) / den
    bb_re = f_re[..., None] * b_re - f_im[..., None] * b_im
    bb_im = f_re[..., None] * b_im + f_im[..., None] * b_re
    eye = jnp.eye(S5_GROUPS, dtype=F32)
    bd_in = lambda bb: jnp.einsum('gpc,gh->gchp', bb, eye).reshape(S5_CH, S5_HID)
    bd_out = lambda cc: jnp.einsum('gcp,gh->gphc', cc, eye).reshape(S5_HID, S5_CH)
    return dict(bbr=bd_in(bb_re).astype(BF16), bbi=bd_in(bb_im).astype(BF16),
                abr=ab_re.reshape(1, S5_HID), abi=ab_im.reshape(1, S5_HID),
                ccr=bd_out(c_re).astype(BF16), ccin=bd_out(-c_im).astype(BF16),
                d=d_skip[None, :], w_glu=w_glu.astype(BF16))


def _tile(n, pref):
    for t in pref:
        if n % t == 0:
            return t
    return n


def _layer_tail(x_tm, layer, gains, ffn, conv_state, *, nb, tm):
    return _ffn(x_tm, gains[layer, 2][None], ffn["w_up"][layer], ffn["conv_w"][layer], ffn["conv_b"][layer][None],
                ffn["w_down"][layer], gains[layer, 3][None], conv_state, nb=nb, tm=tm)


def _run_prompt(x, gains, w0, w1, ws5, ffn):
    nseq, seqlen, _ = x.shape
    rows = nseq * seqlen
    x2d = x.reshape(rows, D_MODEL)
    pos = jnp.arange(seqlen, dtype=jnp.int32)
    tabs = _rope_tables(pos, MLA_ROPE // 2, MLA_HEADS) + _rope_tables(pos, RET_DK // 2, RET_HEADS)
    tm = _tile(seqlen, (512, 256, 128))
    tq = _tile(seqlen, (256, 128))
    qlat, qpe, kfull, rows_new, qr, kr, vr, sg = _inproj0(x2d, gains[0, 0][None], w0, tabs, nseq=nseq,
                                                         seqlen=seqlen, tm=tm, idt=BF16)
    o_mla = _mla_prompt(qlat, qpe, kfull, w0["w_uv"], nseq=nseq, seqlen=seqlen, tq=tq)
    o_ret, s_new = _retention(qr, kr, vr, sg, jnp.zeros((nseq, RET_QK, RET_DV), F32), nseq=nseq, seqlen=seqlen,
                              odt=BF16)
    nt = seqlen // tm
    x1 = _outproj(x2d, o_mla, o_ret, w0["w_out_a"], w0["w_out_b"], gains[0, 1][None], grid=(nseq, nt),
                  x_map=lambda b, i: (b * nt + i, 0), ab_map=lambda b, i: (i, b), y_map=lambda b, i: (i, b),
                  y_shape=(seqlen, nseq * D_MODEL), tm=tm).reshape(rows, D_MODEL)
    tmf = _tile(rows, (512, 256, 128, 64))
    zero_state = jnp.zeros((2 * nseq, 2 * D_FF), F32)
    x2, cs0 = _layer_tail(x1, 0, gains, ffn, zero_state, nb=nseq, tm=tmf)
    u, q, k32, v32, kb, vb = _inproj1(x2, gains[1, 0][None], w1["w_in1"], tm=tmf, idt=BF16)
    tc = _tile(seqlen, (16, 8))
    zero_h = jnp.zeros((nseq, S5_HID), F32)
    o_s5, h_re, h_im = _s5(u.reshape(seqlen, nseq, S5_CH), zero_h, zero_h, ws5, tc=tc, bb=nseq, odt=BF16)
    tm2 = lambda a: a.reshape(seqlen, nseq * SB_W)
    o_sb, k_out, v_out = _sb_prompt(tm2(q), tm2(kb), tm2(vb), tm2(k32), tm2(v32), nseq=nseq, seqlen=seqlen, tq=tq)
    flat = lambda i: (i, 0)
    x3 = _outproj(x2, o_s5.reshape(rows, S5_CH), o_sb.reshape(rows, SB_W), w1["w_out_a"], w1["w_out_b"],
                  gains[1, 1][None], grid=(rows // tmf,), x_map=flat, ab_map=flat, y_map=flat,
                  y_shape=(rows, D_MODEL), tm=tmf)
    x4, cs1 = _layer_tail(x3, 1, gains, ffn, zero_state, nb=nseq, tm=tmf)
    y = _to_seq_major(x4.reshape(seqlen, nseq * D_MODEL), nseq=nseq, seqlen=seqlen, tm=tm)
    conv = jnp.stack([cs0, cs1]).reshape(2, CONV_W - 1, nseq, 2 * D_FF).transpose(0, 2, 1, 3)
    return (y.reshape(nseq, seqlen, D_MODEL), rows_new.reshape(nseq, seqlen, MLA_ROW),
            s_new.reshape(nseq, RET_HEADS, RET_DK, RET_DV), h_re.reshape(nseq, S5_GROUPS, S5_STATE),
            h_im.reshape(nseq, S5_GROUPS, S5_STATE), k_out.reshape(nseq, seqlen, SB_HEADS, SB_DIM),
            v_out.reshape(nseq, seqlen, SB_HEADS, SB_DIM), conv)


def _run_sample(x, gains, w0, w1, ws5, ffn, cache_mla, state_ret, s5_re, s5_im, cache_k, cache_v, state_conv,
                page_table):
    nseq, lnew, _ = x.shape
    rows = nseq * lnew
    past_len = page_table.shape[1] * PAGE_SIZE
    to_tm = lambda a: jnp.swapaxes(a, 0, 1)
    x_tm = to_tm(x).reshape(rows, D_MODEL)
    pos = past_len + jnp.arange(lnew, dtype=jnp.int32)
    pos_rows = jnp.repeat(pos, nseq)
    tabs = _rope_tables(pos_rows, MLA_ROPE // 2, MLA_HEADS) + _rope_tables(pos_rows, RET_DK // 2, RET_HEADS)
    tm = _tile(rows, (512, 256, 128, 64))
    qlat, qpe, _, rows_new, qr, kr, vr, sg = _inproj0(x_tm, gains[0, 0][None], w0, tabs, nseq=1, seqlen=rows,
                                                     tm=tm, idt=F32)
    ql = qlat.reshape(MLA_HEADS, lnew, nseq, MLA_KV_RANK).transpose(2, 0, 1, 3)
    qp = qpe.reshape(lnew, nseq, MLA_HEADS, MLA_ROPE).transpose(1, 2, 0, 3)
    qs = jnp.concatenate([ql, qp], axis=-1).reshape(nseq, MLA_HEADS * lnew, MLA_ROW)
    rows_seq = to_tm(rows_new.reshape(lnew, nseq, MLA_ROW))
    npage = _tile(page_table.shape[1], (16, 8, 4, 2))
    o_mla = _mla_sample(page_table, qs, rows_seq, w0["w_uv"], cache_mla, npage=npage)
    col = lambda a, w: a.reshape(lnew, nseq * w)
    o_ret, s_new = _retention(col(qr, RET_QK), col(kr, RET_QK), col(vr, RET_W), col(sg, RET_W),
                              state_ret.reshape(nseq, RET_QK, RET_DV), nseq=nseq, seqlen=lnew, odt=F32)
    flat = lambda i: (i, 0)
    x1 = _outproj(x_tm, to_tm(o_mla).reshape(rows, MLA_W), o_ret.reshape(rows, RET_W), w0["w_out_a"],
                  w0["w_out_b"], gains[0, 1][None], grid=(rows // tm,), x_map=flat, ab_map=flat, y_map=flat,
                  y_shape=(rows, D_MODEL), tm=tm)
    conv_tm = lambda st: jnp.swapaxes(st, 0, 1).reshape(2 * nseq, 2 * D_FF)
    x2, cs0 = _layer_tail(x1, 0, gains, ffn, conv_tm(state_conv[0]), nb=nseq, tm=rows)
    u, q, k32, v32, _, _ = _inproj1(x2, gains[1, 0][None], w1["w_in1"], tm=tm, idt=F32)
    bb = _tile(nseq, (64, 32, 16, 8))
    o_s5, h_re, h_im = _s5(u.reshape(lnew, nseq, S5_CH), s5_re.reshape(nseq, S5_HID), s5_im.reshape(nseq, S5_HID),
                           ws5, tc=lnew, bb=bb, odt=F32)
    seq = lambda a: to_tm(a.reshape(lnew, nseq, SB_W))
    k_seq, v_seq = seq(k32), seq(v32)
    q_seq = seq(q).reshape(nseq, lnew, SB_HEADS, SB_DIM).transpose(0, 2, 1, 3)
    npage_sb = _tile(page_table.shape[1], (8, 4, 2))
    o_sb = _sb_sample(page_table, q_seq, k_seq, v_seq, cache_k, cache_v, npage=npage_sb)
    o_sb_tm = o_sb.transpose(2, 0, 1, 3).reshape(rows, SB_W)
    x3 = _outproj(x2, o_s5.reshape(rows, S5_CH), o_sb_tm, w1["w_out_a"], w1["w_out_b"], gains[1, 1][None],
                  grid=(rows // tm,), x_map=flat, ab_map=flat, y_map=flat, y_shape=(rows, D_MODEL), tm=tm)
    x4, cs1 = _layer_tail(x3, 1, gains, ffn, conv_tm(state_conv[1]), nb=nseq, tm=rows)
    y = to_tm(x4.reshape(lnew, nseq, D_MODEL))
    conv = jnp.stack([cs0, cs1]).reshape(2, CONV_W - 1, nseq, 2 * D_FF).transpose(0, 2, 1, 3)
    return (y, rows_seq, s_new.reshape(nseq, RET_HEADS, RET_DK, RET_DV), h_re.reshape(nseq, S5_GROUPS, S5_STATE),
            h_im.reshape(nseq, S5_GROUPS, S5_STATE), k_seq.reshape(nseq, lnew, SB_HEADS, SB_DIM),
            v_seq.reshape(nseq, lnew, SB_HEADS, SB_DIM), conv)


def kernel(x_prompt, x_sample, cache_mla, state_ret, state_s5_re, state_s5_im, cache_sb_k, cache_sb_v,
           state_ffn_conv, page_table, norm_gains, w_in_l0, mla_q_norm, mla_w_uq, mla_kv_norm, mla_w_uk,
           mla_w_uv, w_out_l0, w_in_l1, s5_lam_re, s5_lam_im, s5_log_dt, s5_b_re, s5_b_im, s5_c_re, s5_c_im,
           s5_d, s5_w_glu, w_out_l1, ffn_w_up, ffn_conv_w, ffn_conv_b, ffn_w_down):
    w0 = _prep_layer0(w_in_l0, mla_q_norm, mla_w_uq, mla_kv_norm, mla_w_uk, mla_w_uv, w_out_l0)
    w1 = dict(w_in1=w_in_l1.astype(BF16), w_out_a=w_out_l1[:S5_CH].astype(BF16),
              w_out_b=w_out_l1[S5_CH:].astype(BF16))
    ws5 = _prep_s5(s5_lam_re, s5_lam_im, s5_log_dt, s5_b_re, s5_b_im, s5_c_re, s5_c_im, s5_d, s5_w_glu)
    ffn = dict(w_up=ffn_w_up.astype(BF16), conv_w=ffn_conv_w, conv_b=ffn_conv_b, w_down=ffn_w_down.astype(BF16))
    gains = norm_gains.astype(F32)
    (y_p, mla_p, ret_p, s5r_p, s5i_p, sbk_p, sbv_p, conv_p) = _run_prompt(x_prompt, gains, w0, w1, ws5, ffn)
    (y_s, mla_s, ret_s, s5r_s, s5i_s, sbk_s, sbv_s, conv_s) = _run_sample(
        x_sample, gains, w0, w1, ws5, ffn, cache_mla, state_ret, state_s5_re, state_s5_im, cache_sb_k, cache_sb_v,
        state_ffn_conv, page_table)
    return (y_p, y_s, mla_p, mla_s, ret_p, ret_s, s5r_p, s5i_p, s5r_s, s5i_s, sbk_p, sbv_p, sbk_s, sbv_s,
            conv_p, conv_s)
```

```python
import functools
import math

import jax
import jax.numpy as jnp
import numpy as np
from jax import lax
from jax.experimental import pallas as pl
from jax.experimental.pallas import tpu as pltpu

F32 = jnp.float32
BF16 = jnp.bfloat16

D_MODEL = 1024
PAGE_SIZE = 128
MLA_HEADS = 8
MLA_Q_RANK = 384
MLA_KV_RANK = 256
MLA_NOPE = 64
MLA_ROPE = 32
MLA_V = 64
MLA_ROW = MLA_KV_RANK + MLA_ROPE
MLA_W = MLA_HEADS * MLA_V
RET_HEADS = 4
RET_DK = 64
RET_DV = 128
RET_W = RET_HEADS * RET_DV
RET_QK = RET_HEADS * RET_DK
RET_CHUNK = 128
S5_GROUP = 16
S5_CH = 640
S5_GROUPS = S5_CH // S5_GROUP
S5_STATE = 64
S5_HID = S5_GROUPS * S5_STATE
SB_HEADS = 6
SB_DIM = 64
SB_W = SB_HEADS * SB_DIM
D_FF = 2816
CONV_W = 3
ROPE_THETA = 10000.0
EPS = 1e-6
NEG = -1e30

VMEM_LIMIT_BYTES = 56 * 1024 * 1024
LANES = 128


def _cparams(sem):
    return pltpu.CompilerParams(dimension_semantics=sem, vmem_limit_bytes=VMEM_LIMIT_BYTES)


def _const_spec(shape):
    nd = len(shape)
    return pl.BlockSpec(shape, lambda *_: (0,) * nd, pipeline_mode=pl.Buffered(1))


def _rms(x):
    return x * lax.rsqrt(jnp.mean(x * x, axis=-1, keepdims=True) + EPS)


def _gelu_tanh(x):
    c = math.sqrt(2.0 / math.pi)
    return x * (0.5 * (1.0 + jnp.tanh(c * (x + 0.044715 * (x * x * x)))))


def _dot(a, b):
    return jnp.dot(a, b, preferred_element_type=F32)


def _dot_nt(a, b):
    return lax.dot_general(a, b, (((1,), (1,)), ((), ())), preferred_element_type=F32)


def _lane_band(x, lo, width):
    lane = lax.broadcasted_iota(jnp.int32, x.shape, x.ndim - 1)
    return jnp.where((lane >= lo) & (lane < lo + width), x, jnp.zeros_like(x))


_O_CQ, _O_CKV, _O_KPE, _O_KPES = 0, 384, 640, 896
_O_QR, _O_QRS, _O_KR, _O_KRS, _O_VR, _O_GR = 1152, 1408, 1664, 1920, 2176, 2688
IN0_AUG = 3200
_MLA_SCALE = (MLA_NOPE + MLA_ROPE) ** -0.5


def _inproj0_kernel(x_ref, g_ref, win_ref, qn_ref, wuq_ref, kvn_ref, wuk_ref, cm_ref, sm_ref, cr_ref, sr_ref,
                    qlat_ref, qpe_ref, kfull_ref, rows_ref, qr_ref, kr_ref, vr_ref, sg_ref):
    xn = (_rms(x_ref[...]) * g_ref[...]).astype(BF16)
    z = _dot(xn, win_ref[...])
    cqn = (_rms(z[:, _O_CQ:_O_CQ + MLA_Q_RANK]) * qn_ref[...]).astype(BF16)
    q = _dot(cqn, wuq_ref[...])
    for p in range(MLA_HEADS // 2):
        ql = _dot(q[:, 128 * p:128 * (p + 1)].astype(BF16), wuk_ref[p]) * _MLA_SCALE
        qlat_ref[2 * p] = ql[:, :MLA_KV_RANK].astype(qlat_ref.dtype)
        qlat_ref[2 * p + 1] = ql[:, MLA_KV_RANK:].astype(qlat_ref.dtype)
    cm, sm = cm_ref[...], sm_ref[...]
    qpe = (q[:, 512:768] * cm + q[:, 768:1024] * sm) * _MLA_SCALE
    qpe_ref[...] = qpe.astype(qpe_ref.dtype)
    ckv = _rms(z[:, _O_CKV:_O_CKV + MLA_KV_RANK]) * kvn_ref[...]
    kpe = z[:, _O_KPE:_O_KPE + 256] * cm + z[:, _O_KPES:_O_KPES + 256] * sm
    kfull_ref[:, 0:256] = ckv.astype(kfull_ref.dtype)
    kfull_ref[:, 256:512] = kpe.astype(kfull_ref.dtype)
    rows_ref[:, 0:MLA_KV_RANK] = ckv
    rows_ref[:, MLA_KV_RANK:MLA_ROW] = kpe[:, 0:MLA_ROPE]
    cr, sr = cr_ref[...], sr_ref[...]
    qr_ref[...] = z[:, _O_QR:_O_QR + RET_QK] * cr + z[:, _O_QRS:_O_QRS + RET_QK] * sr
    kr_ref[...] = (z[:, _O_KR:_O_KR + RET_QK] * cr + z[:, _O_KRS:_O_KRS + RET_QK] * sr) * (RET_DK ** -0.5)
    vr_ref[...] = z[:, _O_VR:_O_VR + RET_W].astype(vr_ref.dtype)
    g = z[:, _O_GR:_O_GR + RET_W]
    sg_ref[...] = g * jax.nn.sigmoid(g)


def _inproj0(x2d, gain, w, tabs, *, nseq, seqlen, tm, idt):
    nt = seqlen // tm
    cm, sm, cr, sr = tabs
    row_in = lambda b, i: (b * nt + i, 0)
    tm_out = lambda b, i: (i, b)
    tab = lambda b, i: (i, 0)
    out_shape = (
        jax.ShapeDtypeStruct((MLA_HEADS, seqlen, nseq * 256), idt),
        jax.ShapeDtypeStruct((seqlen, nseq * 256), idt),
        jax.ShapeDtypeStruct((seqlen, nseq * 512), idt),
        jax.ShapeDtypeStruct((nseq * seqlen, MLA_ROW), F32),
        jax.ShapeDtypeStruct((seqlen, nseq * RET_QK), F32),
        jax.ShapeDtypeStruct((seqlen, nseq * RET_QK), F32),
        jax.ShapeDtypeStruct((seqlen, nseq * RET_W), idt),
        jax.ShapeDtypeStruct((seqlen, nseq * RET_W), F32),
    )
    out_specs = (
        pl.BlockSpec((MLA_HEADS, tm, 256), lambda b, i: (0, i, b)),
        pl.BlockSpec((tm, 256), tm_out),
        pl.BlockSpec((tm, 512), tm_out),
        pl.BlockSpec((tm, MLA_ROW), row_in),
        pl.BlockSpec((tm, RET_QK), tm_out),
        pl.BlockSpec((tm, RET_QK), tm_out),
        pl.BlockSpec((tm, RET_W), tm_out),
        pl.BlockSpec((tm, RET_W), tm_out),
    )
    in_specs = [
        pl.BlockSpec((tm, D_MODEL), row_in),
        _const_spec((1, D_MODEL)),
        _const_spec((D_MODEL, IN0_AUG)),
        _const_spec((1, MLA_Q_RANK)),
        _const_spec((MLA_Q_RANK, 1024)),
        _const_spec((1, MLA_KV_RANK)),
        _const_spec((MLA_HEADS // 2, 128, 512)),
        pl.BlockSpec((tm, 256), tab), pl.BlockSpec((tm, 256), tab),
        pl.BlockSpec((tm, 256), tab), pl.BlockSpec((tm, 256), tab),
    ]
    return pl.pallas_call(
        _inproj0_kernel, grid=(nseq, nt), in_specs=in_specs, out_specs=out_specs, out_shape=out_shape,
        compiler_params=_cparams(("parallel", "parallel")), name="inproj0",
    )(x2d, gain, w["w_in0"], w["q_norm"], w["w_uq"], w["kv_norm"], w["w_uk"], cm, sm, cr, sr)


def _mla_prompt_kernel(qlat_ref, qpe_ref, k_ref, wuv_ref, o_ref, qs_ref, s_ref, mx_ref, ls_ref, acc_ref, *, tq):
    qi = pl.program_id(1)
    rows = MLA_HEADS * tq
    half = tq // 2
    assert half == LANES
    qpe = qpe_ref[...]
    for h in range(MLA_HEADS):
        qs_ref[h * tq:(h + 1) * tq, 0:256] = qlat_ref[h]
        qs_ref[h * tq:(h + 1) * tq, 256:512] = _lane_band(qpe, MLA_ROPE * h, MLA_ROPE)
    mx_ref[...] = jnp.full((rows, LANES), NEG, F32)
    ls_ref[...] = jnp.zeros((rows, LANES), F32)
    acc_ref[...] = jnp.zeros((rows, MLA_KV_RANK), F32)

    def kblock(j):
        return k_ref[pl.ds(pl.multiple_of(j * tq, tq), tq), :]

    def pass1(j, c):
        s = _dot_nt(qs_ref[...], kblock(j))
        s_ref[j] = s
        mx_ref[...] = jnp.maximum(mx_ref[...], jnp.maximum(s[:, :half], s[:, half:]))
        return c

    lax.fori_loop(0, qi, pass1, 0)
    t_q = lax.broadcasted_iota(jnp.int32, (rows, tq), 0) & (tq - 1)
    t_k = lax.broadcasted_iota(jnp.int32, (rows, tq), 1)
    s = jnp.where(t_k <= t_q, _dot_nt(qs_ref[...], kblock(qi)), NEG)
    s_ref[qi] = s
    mx = jnp.maximum(mx_ref[...], jnp.maximum(s[:, :half], s[:, half:]))
    mx_ref[...] = jnp.broadcast_to(jnp.max(mx, axis=-1, keepdims=True), (rows, LANES))

    def pass2(j, c):
        s = s_ref[j]
        m = mx_ref[...]
        p0 = jnp.exp(s[:, :half] - m)
        p1 = jnp.exp(s[:, half:] - m)
        ls_ref[...] += p0 + p1
        p = jnp.concatenate([p0, p1], axis=1).astype(BF16)
        acc_ref[...] += _dot(p, kblock(j)[:, 0:MLA_KV_RANK])
        return c

    lax.fori_loop(0, qi + 1, pass2, 0)
    l = jnp.sum(ls_ref[...], axis=-1, keepdims=True)
    o_lat = (acc_ref[...] / l).astype(BF16)
    out = _dot(o_lat[0:tq], wuv_ref[0])
    for h in range(1, MLA_HEADS):
        out = out + _dot(o_lat[h * tq:(h + 1) * tq], wuv_ref[h])
    o_ref[...] = out.astype(o_ref.dtype)


def _mla_prompt(qlat, qpe, kfull, wuv, *, nseq, seqlen, tq):
    nq = seqlen // tq
    rows = MLA_HEADS * tq
    assert tq & (tq - 1) == 0
    return pl.pallas_call(
        functools.partial(_mla_prompt_kernel, tq=tq),
        grid=(nseq, nq),
        in_specs=[
            pl.BlockSpec((MLA_HEADS, tq, 256), lambda b, i: (0, i, b)),
            pl.BlockSpec((tq, 256), lambda b, i: (i, b)),
            pl.BlockSpec((seqlen, 512), lambda b, i: (0, b)),
            _const_spec((MLA_HEADS, MLA_KV_RANK, MLA_W)),
        ],
        out_specs=pl.BlockSpec((tq, MLA_W), lambda b, i: (i, b)),
        out_shape=jax.ShapeDtypeStruct((seqlen, nseq * MLA_W), BF16),
        scratch_shapes=[pltpu.VMEM((rows, 512), BF16), pltpu.VMEM((nq, rows, tq), F32),
                        pltpu.VMEM((rows, LANES), F32), pltpu.VMEM((rows, LANES), F32),
                        pltpu.VMEM((rows, MLA_KV_RANK), F32)],
        compiler_params=_cparams(("parallel", "parallel")), name="mla_prompt",
    )(qlat, qpe, kfull, wuv)


def _mla_sample_kernel(pt_ref, qs_ref, knt_ref, wuv_ref, *refs, npage, nstep, lnew):
    pages = refs[:npage]
    o_ref = refs[npage]
    m_ref, l_ref, acc_ref = refs[npage + 1:]
    step = pl.program_id(1)
    rows = MLA_HEADS * lnew

    @pl.when(step == 0)
    def _():
        m_ref[...] = jnp.full((rows, 1), NEG, F32)
        l_ref[...] = jnp.zeros((rows, 1), F32)
        acc_ref[...] = jnp.zeros((rows, MLA_KV_RANK), F32)

    q = qs_ref[0].astype(BF16)

    def update(s, kts):
        m_prev = m_ref[...]
        m_new = jnp.maximum(m_prev, jnp.max(s, axis=-1, keepdims=True))
        alpha = jnp.exp(m_prev - m_new)
        p = jnp.exp(s - m_new)
        l_ref[...] = alpha * l_ref[...] + jnp.sum(p, axis=-1, keepdims=True)
        pv = _dot_nt(p[:, 0:PAGE_SIZE].astype(BF16), kts[0][0:MLA_KV_RANK])
        for n in range(1, len(kts)):
            pv = pv + _dot_nt(p[:, n * PAGE_SIZE:(n + 1) * PAGE_SIZE].astype(BF16), kts[n][0:MLA_KV_RANK])
        acc_ref[...] = alpha * acc_ref[...] + pv
        m_ref[...] = m_new

    kts = [pages[n][0].astype(BF16) for n in range(npage)]
    update(jnp.concatenate([_dot(q, kt) for kt in kts], axis=1), kts)

    @pl.when(step == nstep - 1)
    def _():
        knt = knt_ref[0].astype(BF16)
        sn = _dot(q, knt)
        t_q = lax.broadcasted_iota(jnp.int32, (rows, PAGE_SIZE), 0) & (lnew - 1)
        t_k = lax.broadcasted_iota(jnp.int32, (rows, PAGE_SIZE), 1)
        update(jnp.where(t_k <= t_q, sn, NEG), [knt])
        o_lat = (acc_ref[...] / l_ref[...]).astype(BF16)
        out = _dot(o_lat[0:lnew], wuv_ref[0])
        for h in range(1, MLA_HEADS):
            out = out + _dot(o_lat[h * lnew:(h + 1) * lnew], wuv_ref[h])
        o_ref[0] = out


def _mla_sample(page_table, qs, knew_t, wuv, cache_t, *, npage):
    nseq, n_pages = page_table.shape
    rows = qs.shape[1]
    lnew = rows // MLA_HEADS
    assert lnew & (lnew - 1) == 0 and lnew <= PAGE_SIZE
    nstep = n_pages // npage
    page_specs = [
        pl.BlockSpec((1, MLA_ROW, PAGE_SIZE), lambda b, s, pt, n=n: (pt[b, s * npage + n], 0, 0))
        for n in range(npage)
    ]
    grid_spec = pltpu.PrefetchScalarGridSpec(
        num_scalar_prefetch=1, grid=(nseq, nstep),
        in_specs=[
            pl.BlockSpec((1, rows, MLA_ROW), lambda b, s, pt: (b, 0, 0)),
            pl.BlockSpec((1, MLA_ROW, PAGE_SIZE), lambda b, s, pt: (b, 0, 0)),
            pl.BlockSpec((MLA_HEADS, MLA_KV_RANK, MLA_W), lambda b, s, pt: (0, 0, 0)),
        ] + page_specs,
        out_specs=pl.BlockSpec((1, lnew, MLA_W), lambda b, s, pt: (b, 0, 0)),
        scratch_shapes=[pltpu.VMEM((rows, 1), F32), pltpu.VMEM((rows, 1), F32),
                        pltpu.VMEM((rows, MLA_KV_RANK), F32)],
    )
    return pl.pallas_call(
        functools.partial(_mla_sample_kernel, npage=npage, nstep=nstep, lnew=lnew),
        grid_spec=grid_spec,
        out_shape=jax.ShapeDtypeStruct((nseq, lnew, MLA_W), F32),
        compiler_params=_cparams(("parallel", "arbitrary")), name="mla_sample",
    )(page_table, qs, knew_t, wuv, *([cache_t] * npage))


def _retention_kernel(q_ref, k_ref, v_ref, sg_ref, intra_ref, qdec_ref, kdec_ref, cdec_ref, s0_ref,
                      o_ref, so_ref, s_ref, pad_ref, *, c, nchunk, lrows):
    i = pl.program_id(1)

    @pl.when(i == 0)
    def _():
        s_ref[...] = s0_ref[0]

    row = lax.broadcasted_iota(jnp.int32, (RET_QK, RET_DV), 0)
    for ch in range(nchunk):
        if lrows < c:
            pad_ref[...] = jnp.zeros(pad_ref.shape, F32)
            pad_ref[0:lrows, 0:RET_QK] = q_ref[...]
            pad_ref[0:lrows, RET_QK:2 * RET_QK] = k_ref[...]
            pad_ref[0:lrows, 2 * RET_QK:2 * RET_QK + RET_W] = v_ref[...].astype(F32)
            q = pad_ref[:, 0:RET_QK]
            k = pad_ref[:, RET_QK:2 * RET_QK]
            v_all = pad_ref[:, 2 * RET_QK:2 * RET_QK + RET_W].astype(BF16)
        else:
            rs = slice(ch * c, (ch + 1) * c)
            q, k, v_all = q_ref[rs, :], k_ref[rs, :], v_ref[rs, :].astype(BF16)
        qd = q * qdec_ref[...]
        kdt = (k * kdec_ref[...]).T.astype(BF16)
        kb = k.astype(BF16)
        s_old = s_ref[...]
        sb = s_old.astype(BF16)
        s_new = cdec_ref[...] * s_old
        for h in range(RET_HEADS):
            qh = _lane_band(q, RET_DK * h, RET_DK).astype(BF16)
            qdh = _lane_band(qd, RET_DK * h, RET_DK).astype(BF16)
            vh = v_all[:, RET_DV * h:RET_DV * (h + 1)]
            att = _dot_nt(qh, kb) * intra_ref[h]
            o = _dot(att.astype(BF16), vh) + _dot(qdh, sb)
            o = _rms(o)
            if lrows < c:
                o_ref[:, RET_DV * h:RET_DV * (h + 1)] = (
                    o[0:lrows] * sg_ref[:, RET_DV * h:RET_DV * (h + 1)]).astype(o_ref.dtype)
            else:
                o_ref[rs, RET_DV * h:RET_DV * (h + 1)] = (
                    o * sg_ref[rs, RET_DV * h:RET_DV * (h + 1)]).astype(o_ref.dtype)
            upd = _dot(kdt, vh)
            s_new = s_new + jnp.where((row >= RET_DK * h) & (row < RET_DK * (h + 1)), upd, 0.0)
        s_ref[...] = s_new

    @pl.when(i == pl.num_programs(1) - 1)
    def _():
        so_ref[0] = s_ref[...]


def _retention_tables(c_eff, c_pad):
    log_g = jnp.log(1.0 - 2.0 ** (-5.0 - jnp.arange(RET_HEADS, dtype=F32)))
    i = jnp.arange(c_pad, dtype=F32)
    diff = i[:, None] - i[None, :]
    intra = jnp.where(diff >= 0, jnp.exp(jnp.maximum(diff, 0.0)[None] * log_g[:, None, None]), 0.0)
    q_dec = jnp.exp((i[:, None] + 1.0) * log_g[None, :])
    k_dec = jnp.exp(jnp.maximum(c_eff - 1.0 - i, 0.0)[:, None] * log_g[None, :])
    c_dec = jnp.exp(c_eff * log_g)
    rep = lambda a: jnp.repeat(a, RET_DK, axis=1)
    cdec_rows = jnp.broadcast_to(jnp.repeat(c_dec, RET_DK)[:, None], (RET_QK, RET_DV))
    return intra.astype(F32), rep(q_dec), rep(k_dec), cdec_rows


def _retention(qr, kr, vr, sg, s0, *, nseq, seqlen, odt):
    c = RET_CHUNK
    if seqlen % c == 0:
        c_eff, tr = c, min(seqlen, 4 * c)
        nchunk, lrows = tr // c, tr
    else:
        c_eff, tr, nchunk, lrows = seqlen, seqlen, 1, seqlen
    nt = seqlen // tr
    intra, qdec, kdec, cdec = _retention_tables(c_eff, c)
    blk = lambda w: pl.BlockSpec((tr, w), lambda b, i: (i, b))
    return pl.pallas_call(
        functools.partial(_retention_kernel, c=c, nchunk=nchunk, lrows=lrows),
        grid=(nseq, nt),
        in_specs=[blk(RET_QK), blk(RET_QK), blk(RET_W), blk(RET_W),
                  _const_spec((RET_HEADS, c, c)), _const_spec((c, RET_QK)), _const_spec((c, RET_QK)),
                  _const_spec((RET_QK, RET_DV)),
                  pl.BlockSpec((1, RET_QK, RET_DV), lambda b, i: (b, 0, 0))],
        out_specs=(blk(RET_W), pl.BlockSpec((1, RET_QK, RET_DV), lambda b, i: (b, 0, 0))),
        out_shape=(jax.ShapeDtypeStruct((seqlen, nseq * RET_W), odt),
                   jax.ShapeDtypeStruct((nseq, RET_QK, RET_DV), F32)),
        scratch_shapes=[pltpu.VMEM((RET_QK, RET_DV), F32), pltpu.VMEM((c, 2 * RET_QK + RET_W), F32)],
        compiler_params=_cparams(("parallel", "arbitrary")), name="retention",
    )(qr, kr, vr, sg, intra, qdec, kdec, cdec, s0)


def _outproj_kernel(x_ref, a_ref, b_ref, wa_ref, wb_ref, g_ref, y_ref):
    m = _dot(a_ref[...].astype(BF16), wa_ref[...]) + _dot(b_ref[...].astype(BF16), wb_ref[...])
    y_ref[...] = x_ref[...] + _rms(m) * g_ref[...]


def _outproj(x, a, b, wa, wb, gain, *, grid, x_map, ab_map, y_map, y_shape, tm):
    wa_w, wb_w = wa.shape[0], wb.shape[0]
    return pl.pallas_call(
        _outproj_kernel, grid=grid,
        in_specs=[pl.BlockSpec((tm, D_MODEL), x_map), pl.BlockSpec((tm, wa_w), ab_map),
                  pl.BlockSpec((tm, wb_w), ab_map), _const_spec(wa.shape), _const_spec(wb.shape),
                  _const_spec((1, D_MODEL))],
        out_specs=pl.BlockSpec((tm, D_MODEL), y_map),
        out_shape=jax.ShapeDtypeStruct(y_shape, F32),
        compiler_params=_cparams(("parallel",) * len(grid)), name="outproj",
    )(x, a, b, wa, wb, gain)


_FFN_CK = 256


def _ffn_kernel(x_ref, g2_ref, wup_ref, cw_ref, cb_ref, wdn_ref, g3_ref, st_ref, y_ref, cs_ref,
                carry_ref, hp_ref, acc_ref, *, nb, tm):
    i = pl.program_id(0)
    ck = _FFN_CK

    @pl.when(i == 0)
    def _():
        carry_ref[...] = st_ref[...]

    x = x_ref[...]
    xn = (_rms(x) * g2_ref[...]).astype(BF16)
    acc_ref[...] = jnp.zeros((tm, D_MODEL), F32)
    for c in range(D_FF // ck):
        act = []
        for half in range(2):
            col = half * D_FF + c * ck
            cs = slice(col, col + ck)
            h = _dot(xn, wup_ref[:, cs])
            hp_ref[0:2 * nb, :] = carry_ref[:, cs]
            hp_ref[2 * nb:2 * nb + tm, :] = h
            carry_ref[:, cs] = hp_ref[tm:tm + 2 * nb, :]
            act.append(cb_ref[:, cs] + cw_ref[0:1, cs] * hp_ref[0:tm, :] + cw_ref[1:2, cs] * hp_ref[nb:nb + tm, :]
                       + cw_ref[2:3, cs] * h)
        a = (_gelu_tanh(act[0]) * act[1]).astype(BF16)
        acc_ref[...] += _dot(a, wdn_ref[c * ck:(c + 1) * ck, :])
    y_ref[...] = x + _rms(acc_ref[...]) * g3_ref[...]

    @pl.when(i == pl.num_programs(0) - 1)
    def _():
        cs_ref[...] = carry_ref[...]


def _ffn(x, g2, wup, cw, cb, wdn, g3, state, *, nb, tm):
    rows = x.shape[0]
    assert tm >= 2 * nb and tm % nb == 0 and rows % tm == 0
    blk = pl.BlockSpec((tm, D_MODEL), lambda i: (i, 0))
    return pl.pallas_call(
        functools.partial(_ffn_kernel, nb=nb, tm=tm),
        grid=(rows // tm,),
        in_specs=[blk, _const_spec((1, D_MODEL)), _const_spec((D_MODEL, 2 * D_FF)), _const_spec((CONV_W, 2 * D_FF)),
                  _const_spec((1, 2 * D_FF)), _const_spec((D_FF, D_MODEL)), _const_spec((1, D_MODEL)),
                  _const_spec((2 * nb, 2 * D_FF))],
        out_specs=(blk, pl.BlockSpec((2 * nb, 2 * D_FF), lambda i: (0, 0))),
        out_shape=(jax.ShapeDtypeStruct((rows, D_MODEL), F32), jax.ShapeDtypeStruct((2 * nb, 2 * D_FF), F32)),
        scratch_shapes=[pltpu.VMEM((2 * nb, 2 * D_FF), F32), pltpu.VMEM((tm + 2 * nb, _FFN_CK), F32),
                        pltpu.VMEM((tm, D_MODEL), F32)],
        compiler_params=_cparams(("arbitrary",)), name="convffn",
    )(x, g2, wup, cw, cb, wdn, g3, state)


def _inproj1_kernel(x_ref, g_ref, win_ref, u_ref, q_ref, k_ref, v_ref, kb_ref, vb_ref):
    xn = (_rms(x_ref[...]) * g_ref[...]).astype(BF16)
    z = _dot(xn, win_ref[...])
    u_ref[...] = z[:, 0:S5_CH]
    q_ref[...] = (z[:, S5_CH:S5_CH + SB_W] * (SB_DIM ** -0.5)).astype(q_ref.dtype)
    k = z[:, S5_CH + SB_W:S5_CH + 2 * SB_W]
    v = z[:, S5_CH + 2 * SB_W:S5_CH + 3 * SB_W]
    k_ref[...] = k
    v_ref[...] = v
    kb_ref[...] = k.astype(kb_ref.dtype)
    vb_ref[...] = v.astype(vb_ref.dtype)


def _inproj1(x, gain, win, *, tm, idt):
    rows = x.shape[0]
    blk = lambda w: pl.BlockSpec((tm, w), lambda i: (i, 0))
    sds = lambda w, dt: jax.ShapeDtypeStruct((rows, w), dt)
    return pl.pallas_call(
        _inproj1_kernel, grid=(rows // tm,),
        in_specs=[blk(D_MODEL), _const_spec((1, D_MODEL)), _const_spec(win.shape)],
        out_specs=(blk(S5_CH), blk(SB_W), blk(SB_W), blk(SB_W), blk(SB_W), blk(SB_W)),
        out_shape=(sds(S5_CH, F32), sds(SB_W, idt), sds(SB_W, F32), sds(SB_W, F32), sds(SB_W, idt), sds(SB_W, idt)),
        compiler_params=_cparams(("parallel",)), name="inproj1",
    )(x, gain, win)


_S5_LW = 256
_S5_CB = 128
_S5_NBLK = S5_CH // _S5_CB
_S5_HB = S5_HID // _S5_NBLK


def _s5_kernel(u_ref, h0r_ref, h0i_ref, br_ref, bi_ref, ar_ref, ai_ref, cr_ref, cin_ref, d_ref, wg_ref,
               o_ref, hr_ref, hi_ref, sr_ref, si_ref, *, tc, bb):
    i = pl.program_id(1)

    @pl.when(i == 0)
    def _():
        hr_ref[...] = h0r_ref[...]
        hi_ref[...] = h0i_ref[...]

    u = u_ref[...].reshape(tc * bb, S5_CH)
    ub = u.astype(BF16)
    for m in range(_S5_NBLK):
        um = ub[:, _S5_CB * m:_S5_CB * (m + 1)]
        sr_ref[:, _S5_HB * m:_S5_HB * (m + 1)] = _dot(um, br_ref[m])
        si_ref[:, _S5_HB * m:_S5_HB * (m + 1)] = _dot(um, bi_ref[m])
    for cb in range(S5_HID // _S5_LW):
        cs = slice(cb * _S5_LW, (cb + 1) * _S5_LW)
        ar, ai = ar_ref[:, cs], ai_ref[:, cs]

        def body(t, carry, cs=cs, ar=ar, ai=ai):
            hr, hi = carry
            rs = pl.ds(pl.multiple_of(t * bb, bb), bb)
            nr = ar * hr - ai * hi + sr_ref[rs, cs]
            ni = ar * hi + ai * hr + si_ref[rs, cs]
            sr_ref[rs, cs] = nr
            si_ref[rs, cs] = ni
            return nr, ni

        hr, hi = lax.fori_loop(0, tc, body, (hr_ref[:, cs], hi_ref[:, cs]))
        hr_ref[:, cs] = hr
        hi_ref[:, cs] = hi
    ys = []
    for m in range(_S5_NBLK):
        hs = slice(_S5_HB * m, _S5_HB * (m + 1))
        ys.append(_dot(sr_ref[:, hs].astype(BF16), cr_ref[m]) + _dot(si_ref[:, hs].astype(BF16), cin_ref[m]))
    y = jnp.concatenate(ys, axis=1)
    y = _gelu_tanh(y + d_ref[...] * u)
    o = y * jax.nn.sigmoid(_dot(y.astype(BF16), wg_ref[...]))
    o_ref[...] = o.reshape(tc, bb, S5_CH).astype(o_ref.dtype)


def _s5(u3, h0r, h0i, w, *, tc, bb, odt):
    tlen, nseq, _ = u3.shape
    ublk = pl.BlockSpec((tc, bb, S5_CH), lambda j, i: (i, j, 0))
    hblk = pl.BlockSpec((bb, S5_HID), lambda j, i: (j, 0))
    return pl.pallas_call(
        functools.partial(_s5_kernel, tc=tc, bb=bb),
        grid=(nseq // bb, tlen // tc),
        in_specs=[ublk, hblk, hblk, _const_spec((_S5_NBLK, _S5_CB, _S5_HB)), _const_spec((_S5_NBLK, _S5_CB, _S5_HB)),
                  _const_spec((1, S5_HID)), _const_spec((1, S5_HID)), _const_spec((_S5_NBLK, _S5_HB, _S5_CB)),
                  _const_spec((_S5_NBLK, _S5_HB, _S5_CB)), _const_spec((1, S5_CH)), _const_spec((S5_CH, S5_CH))],
        out_specs=(ublk, hblk, hblk),
        out_shape=(jax.ShapeDtypeStruct((tlen, nseq, S5_CH), odt), jax.ShapeDtypeStruct((nseq, S5_HID), F32),
                   jax.ShapeDtypeStruct((nseq, S5_HID), F32)),
        scratch_shapes=[pltpu.VMEM((tc * bb, S5_HID), F32), pltpu.VMEM((tc * bb, S5_HID), F32)],
        compiler_params=_cparams(("parallel", "arbitrary")), name="s5",
    )(u3, h0r, h0i, w["bbr"], w["bbi"], w["abr"], w["abi"], w["ccr"], w["ccin"], w["d"], w["w_glu"])


def _sb_weights(z, tri, carry, valid):
    t = jnp.log(1.0 + jnp.exp(-jnp.abs(z)))
    lsz = jnp.minimum(z, 0.0) - t
    lk = lsz - z
    if valid is not None:
        lk = jnp.where(valid, lk, 0.0)
    hi = lk.astype(BF16)
    lo = (lk - hi.astype(F32)).astype(BF16)
    later = _dot(hi, tri) + _dot(lo, tri) + carry
    w = jnp.exp(lsz + later)
    if valid is not None:
        w = jnp.where(valid, w, 0.0)
    return w.astype(BF16), jnp.sum(lk, axis=-1, keepdims=True)


def _sb_prompt_kernel(q_ref, kb_ref, vb_ref, k32_ref, v32_ref, tri_ref, o_ref, ko_ref, vo_ref,
                      qm_ref, carry_ref, acc_ref, *, tq):
    qi = pl.program_id(1)
    rows = SB_HEADS * tq
    ko_ref[...] = k32_ref[...]
    vo_ref[...] = v32_ref[...]
    tri = tri_ref[...]
    for h in range(SB_HEADS):
        qm_ref[h] = _lane_band(q_ref[:, LANES * (h // 2):LANES * (h // 2 + 1)], SB_DIM * (h % 2), SB_DIM)
    carry_ref[...] = jnp.zeros(carry_ref.shape, F32)
    acc_ref[...] = jnp.zeros(acc_ref.shape, F32)

    def visit(j, valid):
        rs = pl.ds(pl.multiple_of(j * tq, tq), tq)
        ps = [slice(LANES * (h // 2), LANES * (h // 2 + 1)) for h in range(SB_HEADS)]
        z = jnp.concatenate([_dot_nt(qm_ref[h], kb_ref[rs, ps[h]]) for h in range(SB_HEADS)], axis=0)
        w, lksum = _sb_weights(z, tri, carry_ref[...], valid)
        for h in range(SB_HEADS):
            acc_ref[h] += _dot(w[h * tq:(h + 1) * tq], vb_ref[rs, ps[h]])
        carry_ref[...] += lksum

    t_q = lax.broadcasted_iota(jnp.int32, (rows, tq), 0) & (tq - 1)
    t_k = lax.broadcasted_iota(jnp.int32, (rows, tq), 1)
    visit(qi, t_k < t_q)

    def body(jj, c):
        visit(qi - 1 - jj, None)
        return c

    lax.fori_loop(0, qi, body, 0)
    lane = lax.broadcasted_iota(jnp.int32, (tq, LANES), 1)
    for p in range(SB_HEADS // 2):
        o_ref[:, LANES * p:LANES * (p + 1)] = jnp.where(
            lane < SB_DIM, acc_ref[2 * p], acc_ref[2 * p + 1]).astype(o_ref.dtype)


def _tri(n):
    j = jnp.arange(n)
    return (j[:, None] > j[None, :]).astype(BF16)


def _sb_prompt(q, kb, vb, k32, v32, *, nseq, seqlen, tq):
    nq = seqlen // tq
    colblk = pl.BlockSpec((tq, SB_W), lambda b, i: (i, b))
    seqblk = pl.BlockSpec((seqlen, SB_W), lambda b, i: (0, b))
    rowblk = pl.BlockSpec((tq, SB_W), lambda b, i: (b * nq + i, 0))
    return pl.pallas_call(
        functools.partial(_sb_prompt_kernel, tq=tq),
        grid=(nseq, nq),
        in_specs=[colblk, seqblk, seqblk, colblk, colblk, _const_spec((tq, tq))],
        out_specs=(colblk, rowblk, rowblk),
        out_shape=(jax.ShapeDtypeStruct((seqlen, nseq * SB_W), BF16),
                   jax.ShapeDtypeStruct((nseq * seqlen, SB_W), F32),
                   jax.ShapeDtypeStruct((nseq * seqlen, SB_W), F32)),
        scratch_shapes=[pltpu.VMEM((SB_HEADS, tq, LANES), BF16), pltpu.VMEM((SB_HEADS * tq, 1), F32),
                        pltpu.VMEM((SB_HEADS, tq, LANES), F32)],
        compiler_params=_cparams(("parallel", "parallel")), name="sb_prompt",
    )(q, kb, vb, k32, v32, _tri(tq))


def _sb_sample_kernel(pt_ref, q_ref, knt_ref, vnt_ref, tri_ref, *refs, npage, nstep, lnew):
    kpages = refs[:npage]
    vpages = refs[npage:2 * npage]
    o_ref = refs[2 * npage]
    carry_ref, acc_ref = refs[2 * npage + 1:]
    step = pl.program_id(1)
    rows = SB_HEADS * lnew
    tri = tri_ref[...]
    q = q_ref[0].astype(BF16)

    @pl.when(step == 0)
    def _():
        t_q = lax.broadcasted_iota(jnp.int32, (rows, PAGE_SIZE), 0) & (lnew - 1)
        t_k = lax.broadcasted_iota(jnp.int32, (rows, PAGE_SIZE), 1)
        w, lksum = _sb_weights(_dot(q, knt_ref[0].astype(BF16)), tri, jnp.zeros((rows, 1), F32), t_k < t_q)
        acc_ref[...] = _dot_nt(w, vnt_ref[0].astype(BF16))
        carry_ref[...] = lksum

    z = jnp.concatenate([_dot(q, kpages[n][0].reshape(SB_W, PAGE_SIZE).astype(BF16)) for n in range(npage)],
                        axis=0)
    t = jnp.log(1.0 + jnp.exp(-jnp.abs(z)))
    lsz = jnp.minimum(z, 0.0) - t
    lk = lsz - z
    hi = lk.astype(BF16)
    lo = (lk - hi.astype(F32)).astype(BF16)
    cum = _dot(hi, tri) + _dot(lo, tri)
    lksum = jnp.sum(lk, axis=-1, keepdims=True)
    carries, carry = [], carry_ref[...]
    for n in range(npage):
        carries.append(carry)
        carry = carry + lksum[n * rows:(n + 1) * rows]
    carry_ref[...] = carry
    w = jnp.exp(lsz + cum + jnp.concatenate(carries, axis=0)).astype(BF16)
    pv = _dot_nt(w[0:rows], vpages[0][0].reshape(SB_W, PAGE_SIZE).astype(BF16))
    for n in range(1, npage):
        pv = pv + _dot_nt(w[n * rows:(n + 1) * rows], vpages[n][0].reshape(SB_W, PAGE_SIZE).astype(BF16))
    acc_ref[...] += pv

    @pl.when(step == nstep - 1)
    def _():
        acc = acc_ref[...]
        out = _lane_band(acc[0:lnew], 0, SB_DIM)
        for h in range(1, SB_HEADS):
            out = out + _lane_band(acc[h * lnew:(h + 1) * lnew], SB_DIM * h, SB_DIM)
        o_ref[0] = out


def _sb_sample(page_table, q_bd, knew_t, vnew_t, cache_kt, cache_vt, *, npage):
    nseq, n_pages = page_table.shape
    rows = q_bd.shape[1]
    lnew = rows // SB_HEADS
    assert lnew & (lnew - 1) == 0 and lnew <= PAGE_SIZE
    nstep = n_pages // npage
    pidx = lambda b, s, pt, n: (pt[b, n_pages - 1 - (s * npage + n)], 0, 0, 0)
    pspecs = [pl.BlockSpec((1, SB_HEADS, SB_DIM, PAGE_SIZE), functools.partial(pidx, n=n)) for n in range(npage)]
    grid_spec = pltpu.PrefetchScalarGridSpec(
        num_scalar_prefetch=1, grid=(nseq, nstep),
        in_specs=[
            pl.BlockSpec((1, rows, SB_W), lambda b, s, pt: (b, 0, 0)),
            pl.BlockSpec((1, SB_W, PAGE_SIZE), lambda b, s, pt: (b, 0, 0)),
            pl.BlockSpec((1, SB_W, PAGE_SIZE), lambda b, s, pt: (b, 0, 0)),
            pl.BlockSpec((PAGE_SIZE, PAGE_SIZE), lambda b, s, pt: (0, 0)),
        ] + pspecs + pspecs,
        out_specs=pl.BlockSpec((1, lnew, SB_W), lambda b, s, pt: (b, 0, 0)),
        scratch_shapes=[pltpu.VMEM((rows, 1), F32), pltpu.VMEM((rows, SB_W), F32)],
    )
    return pl.pallas_call(
        functools.partial(_sb_sample_kernel, npage=npage, nstep=nstep, lnew=lnew),
        grid_spec=grid_spec,
        out_shape=jax.ShapeDtypeStruct((nseq, lnew, SB_W), F32),
        compiler_params=_cparams(("parallel", "arbitrary")), name="sb_sample",
    )(page_table, q_bd, knew_t, vnew_t, _tri(PAGE_SIZE), *([cache_kt] * npage), *([cache_vt] * npage))


def _copy_kernel(x_ref, y_ref):
    y_ref[...] = x_ref[...]


def _to_seq_major(x_tm, *, nseq, seqlen, tm):
    nt = seqlen // tm
    return pl.pallas_call(
        _copy_kernel, grid=(nseq, nt),
        in_specs=[pl.BlockSpec((tm, D_MODEL), lambda b, i: (i, b))],
        out_specs=pl.BlockSpec((tm, D_MODEL), lambda b, i: (b * nt + i, 0)),
        out_shape=jax.ShapeDtypeStruct((nseq * seqlen, D_MODEL), F32),
        compiler_params=_cparams(("parallel", "parallel")), name="to_seq_major",
    )(x_tm)


def _swap_halves(n_heads, dim):
    idx = np.arange(n_heads * dim)
    return (idx // dim) * dim + (idx % dim + dim // 2) % dim


def _rope_tables(pos, half, reps):
    freqs = ROPE_THETA ** (-jnp.arange(half, dtype=F32) / half)
    ang = pos.astype(F32)[:, None] * freqs[None, :]
    c, s = jnp.cos(ang), jnp.sin(ang)
    return (jnp.tile(jnp.concatenate([c, c], axis=-1), (1, reps)),
            jnp.tile(jnp.concatenate([-s, s], axis=-1), (1, reps)))


def _prep_layer0(w_in, q_norm, w_uq, kv_norm, w_uk, w_uv, w_out):
    o_kpe = MLA_Q_RANK + MLA_KV_RANK
    o_qr = o_kpe + MLA_ROPE
    o_kr = o_qr + RET_QK
    o_v = o_kr + RET_QK
    cols = np.concatenate([
        np.arange(0, o_kpe),
        np.tile(o_kpe + np.arange(MLA_ROPE), MLA_HEADS), np.tile(o_kpe + _swap_halves(1, MLA_ROPE), MLA_HEADS),
        o_qr + np.arange(RET_QK), o_qr + _swap_halves(RET_HEADS, RET_DK),
        o_kr + np.arange(RET_QK), o_kr + _swap_halves(RET_HEADS, RET_DK),
        np.arange(o_v, o_v + 2 * RET_W)])
    assert cols.shape[0] == IN0_AUG
    hd = MLA_NOPE + MLA_ROPE
    heads = np.arange(MLA_HEADS)[:, None] * hd
    uq_cols = np.concatenate([
        (heads + np.arange(MLA_NOPE)[None, :]).ravel(),
        (heads + MLA_NOPE + np.arange(MLA_ROPE)[None, :]).ravel(),
        (heads + MLA_NOPE + _swap_halves(1, MLA_ROPE)[None, :]).ravel()])
    wuk_t = jnp.transpose(w_uk, (1, 2, 0))
    z = jnp.zeros((MLA_NOPE, MLA_KV_RANK), F32)
    wuk_pair = jnp.stack([jnp.block([[wuk_t[2 * p], z], [z, wuk_t[2 * p + 1]]]) for p in range(MLA_HEADS // 2)])
    wuv_pad = jnp.zeros((MLA_HEADS, MLA_KV_RANK, MLA_W), F32)
    for h in range(MLA_HEADS):
        wuv_pad = wuv_pad.at[h, :, MLA_V * h:MLA_V * (h + 1)].set(w_uv[:, h, :])
    return dict(w_in0=w_in[:, cols].astype(BF16), q_norm=q_norm[None, :], w_uq=w_uq[:, uq_cols].astype(BF16),
                kv_norm=kv_norm[None, :], w_uk=wuk_pair.astype(BF16), w_uv=wuv_pad.astype(BF16),
                w_out_a=w_out[:MLA_W].astype(BF16), w_out_b=w_out[MLA_W:].astype(BF16))


def _prep_s5(lam_re, lam_im, log_dt, b_re, b_im, c_re, c_im, d_skip, w_glu):
    lr, li = lam_re.astype(F32), lam_im.astype(F32)
    dt = jnp.exp(log_dt.astype(F32))[:, None]
    mag = jnp.exp(lr * dt)
    ab_re, ab_im = mag * jnp.cos(li * dt), mag * jnp.sin(li * dt)
    nr, ni = ab_re - 1.0, ab_im
    den = lr * lr + li * li
    f_re = (nr * lr + ni * li) / den
    f_im = (ni * lr - nr * li) / den
    bb_re = f_re[..., None] * b_re - f_im[..., None] * b_im
    bb_im = f_re[..., None] * b_im + f_im[..., None] * b_re
    gpb = _S5_CB // S5_GROUP
    eye = jnp.eye(gpb, dtype=F32)
    bd_in = lambda bb: jnp.einsum('mgpc,gh->mgchp', bb.reshape(_S5_NBLK, gpb, S5_STATE, S5_GROUP),
                                  eye).reshape(_S5_NBLK, _S5_CB, _S5_HB)
    bd_out = lambda cc: jnp.einsum('mgcp,gh->mgphc', cc.reshape(_S5_NBLK, gpb, S5_GROUP, S5_STATE),
                                   eye).reshape(_S5_NBLK, _S5_HB, _S5_CB)
    return dict(bbr=bd_in(bb_re).astype(BF16), bbi=bd_in(bb_im).astype(BF16),
                abr=ab_re.reshape(1, S5_HID), abi=ab_im.reshape(1, S5_HID),
                ccr=bd_out(c_re).astype(BF16), ccin=bd_out(-c_im).astype(BF16),
                d=d_skip[None, :], w_glu=w_glu.astype(BF16))


def _tile(n, pref):
    for t in pref:
        if n % t == 0:
            return t
    return n


def _layer_tail(x_tm, layer, gains, ffn, conv_state, *, nb, tm):
    return _ffn(x_tm, gains[layer, 2][None], ffn["w_up"][layer], ffn["conv_w"][layer], ffn["conv_b"][layer][None],
                ffn["w_down"][layer], gains[layer, 3][None], conv_state, nb=nb, tm=tm)


def _run_prompt(x, gains, w0, w1, ws5, ffn):
    nseq, seqlen, _ = x.shape
    rows = nseq * seqlen
    x2d = x.reshape(rows, D_MODEL)
    pos = jnp.arange(seqlen, dtype=jnp.int32)
    tabs = _rope_tables(pos, MLA_ROPE // 2, MLA_HEADS) + _rope_tables(pos, RET_DK // 2, RET_HEADS)
    tm = _tile(seqlen, (512, 256, 128))
    tq = _tile(seqlen, (256, 128))
    qlat, qpe, kfull, rows_new, qr, kr, vr, sg = _inproj0(x2d, gains[0, 0][None], w0, tabs, nseq=nseq,
                                                         seqlen=seqlen, tm=tm, idt=BF16)
    o_mla = _mla_prompt(qlat, qpe, kfull, w0["w_uv"], nseq=nseq, seqlen=seqlen, tq=tq)
    o_ret, s_new = _retention(qr, kr, vr, sg, jnp.zeros((nseq, RET_QK, RET_DV), F32), nseq=nseq, seqlen=seqlen,
                              odt=BF16)
    nt = seqlen // tm
    x1 = _outproj(x2d, o_mla, o_ret, w0["w_out_a"], w0["w_out_b"], gains[0, 1][None], grid=(nseq, nt),
                  x_map=lambda b, i: (b * nt + i, 0), ab_map=lambda b, i: (i, b), y_map=lambda b, i: (i, b),
                  y_shape=(seqlen, nseq * D_MODEL), tm=tm).reshape(rows, D_MODEL)
    tmf = _tile(rows, (512, 256, 128, 64))
    zero_state = jnp.zeros((2 * nseq, 2 * D_FF), F32)
    x2, cs0 = _layer_tail(x1, 0, gains, ffn, zero_state, nb=nseq, tm=tmf)
    u, q, k32, v32, kb, vb = _inproj1(x2, gains[1, 0][None], w1["w_in1"], tm=tmf, idt=BF16)
    tc = _tile(seqlen, (16, 8))
    zero_h = jnp.zeros((nseq, S5_HID), F32)
    o_s5, h_re, h_im = _s5(u.reshape(seqlen, nseq, S5_CH), zero_h, zero_h, ws5, tc=tc, bb=nseq, odt=BF16)
    tm2 = lambda a: a.reshape(seqlen, nseq * SB_W)
    o_sb, k_out, v_out = _sb_prompt(tm2(q), tm2(kb), tm2(vb), tm2(k32), tm2(v32), nseq=nseq, seqlen=seqlen, tq=tq)
    flat = lambda i: (i, 0)
    x3 = _outproj(x2, o_s5.reshape(rows, S5_CH), o_sb.reshape(rows, SB_W), w1["w_out_a"], w1["w_out_b"],
                  gains[1, 1][None], grid=(rows // tmf,), x_map=flat, ab_map=flat, y_map=flat,
                  y_shape=(rows, D_MODEL), tm=tmf)
    x4, cs1 = _layer_tail(x3, 1, gains, ffn, zero_state, nb=nseq, tm=tmf)
    y = _to_seq_major(x4.reshape(seqlen, nseq * D_MODEL), nseq=nseq, seqlen=seqlen, tm=tm)
    conv = jnp.stack([cs0, cs1]).reshape(2, CONV_W - 1, nseq, 2 * D_FF).transpose(0, 2, 1, 3)
    return (y.reshape(nseq, seqlen, D_MODEL), rows_new.reshape(nseq, seqlen, MLA_ROW),
            s_new.reshape(nseq, RET_HEADS, RET_DK, RET_DV), h_re.reshape(nseq, S5_GROUPS, S5_STATE),
            h_im.reshape(nseq, S5_GROUPS, S5_STATE), k_out.reshape(nseq, seqlen, SB_HEADS, SB_DIM),
            v_out.reshape(nseq, seqlen, SB_HEADS, SB_DIM), conv)


def _run_sample(x, gains, w0, w1, ws5, ffn, cache_mla, state_ret, s5_re, s5_im, cache_k, cache_v, state_conv,
                page_table):
    nseq, lnew, _ = x.shape
    rows = nseq * lnew
    past_len = page_table.shape[1] * PAGE_SIZE
    to_tm = lambda a: jnp.swapaxes(a, 0, 1)
    x_tm = to_tm(x).reshape(rows, D_MODEL)
    pos = past_len + jnp.arange(lnew, dtype=jnp.int32)
    pos_rows = jnp.repeat(pos, nseq)
    tabs = _rope_tables(pos_rows, MLA_ROPE // 2, MLA_HEADS) + _rope_tables(pos_rows, RET_DK // 2, RET_HEADS)
    tm = _tile(rows, (512, 256, 128, 64))
    qlat, qpe, _, rows_new, qr, kr, vr, sg = _inproj0(x_tm, gains[0, 0][None], w0, tabs, nseq=1, seqlen=rows,
                                                     tm=tm, idt=F32)
    ql = qlat.reshape(MLA_HEADS, lnew, nseq, MLA_KV_RANK).transpose(2, 0, 1, 3)
    qp = qpe.reshape(lnew, nseq, MLA_HEADS, MLA_ROPE).transpose(1, 2, 0, 3)
    qs = jnp.concatenate([ql, qp], axis=-1).reshape(nseq, MLA_HEADS * lnew, MLA_ROW)
    rows_seq = to_tm(rows_new.reshape(lnew, nseq, MLA_ROW))
    pad_keys = lambda a: jnp.pad(jnp.swapaxes(a, 1, 2), ((0, 0), (0, 0), (0, PAGE_SIZE - lnew)))
    npage = _tile(page_table.shape[1], (16, 8, 4, 2))
    o_mla = _mla_sample(page_table, qs, pad_keys(rows_seq), w0["w_uv"], jnp.transpose(cache_mla, (0, 2, 1)),
                        npage=npage)
    col = lambda a, w: a.reshape(lnew, nseq * w)
    o_ret, s_new = _retention(col(qr, RET_QK), col(kr, RET_QK), col(vr, RET_W), col(sg, RET_W),
                              state_ret.reshape(nseq, RET_QK, RET_DV), nseq=nseq, seqlen=lnew, odt=F32)
    flat = lambda i: (i, 0)
    x1 = _outproj(x_tm, to_tm(o_mla).reshape(rows, MLA_W), o_ret.reshape(rows, RET_W), w0["w_out_a"],
                  w0["w_out_b"], gains[0, 1][None], grid=(rows // tm,), x_map=flat, ab_map=flat, y_map=flat,
                  y_shape=(rows, D_MODEL), tm=tm)
    conv_tm = lambda st: jnp.swapaxes(st, 0, 1).reshape(2 * nseq, 2 * D_FF)
    x2, cs0 = _layer_tail(x1, 0, gains, ffn, conv_tm(state_conv[0]), nb=nseq, tm=rows)
    u, q, k32, v32, _, _ = _inproj1(x2, gains[1, 0][None], w1["w_in1"], tm=tm, idt=F32)
    bb = _tile(nseq, (64, 32, 16, 8))
    o_s5, h_re, h_im = _s5(u.reshape(lnew, nseq, S5_CH), s5_re.reshape(nseq, S5_HID), s5_im.reshape(nseq, S5_HID),
                           ws5, tc=lnew, bb=bb, odt=F32)
    seq = lambda a: to_tm(a.reshape(lnew, nseq, SB_W))
    k_seq, v_seq = seq(k32), seq(v32)
    q_hd = seq(q).reshape(nseq, lnew, SB_HEADS, SB_DIM).transpose(0, 2, 1, 3)
    q_bd = jnp.einsum('bhtd,hg->bhtgd', q_hd, jnp.eye(SB_HEADS, dtype=F32)).reshape(nseq, SB_HEADS * lnew, SB_W)
    npage_sb = _tile(page_table.shape[1], (16, 8, 4, 2))
    o_sb = _sb_sample(page_table, q_bd, pad_keys(k_seq), pad_keys(v_seq), jnp.transpose(cache_k, (0, 2, 3, 1)),
                      jnp.transpose(cache_v, (0, 2, 3, 1)), npage=npage_sb)
    o_sb_tm = to_tm(o_sb).reshape(rows, SB_W)
    x3 = _outproj(x2, o_s5.reshape(rows, S5_CH), o_sb_tm, w1["w_out_a"], w1["w_out_b"], gains[1, 1][None],
                  grid=(rows // tm,), x_map=flat, ab_map=flat, y_map=flat, y_shape=(rows, D_MODEL), tm=tm)
    x4, cs1 = _layer_tail(x3, 1, gains, ffn, conv_tm(state_conv[1]), nb=nseq, tm=rows)
    y = to_tm(x4.reshape(lnew, nseq, D_MODEL))
    conv = jnp.stack([cs0, cs1]).reshape(2, CONV_W - 1, nseq, 2 * D_FF).transpose(0, 2, 1, 3)
    return (y, rows_seq, s_new.reshape(nseq, RET_HEADS, RET_DK, RET_DV), h_re.reshape(nseq, S5_GROUPS, S5_STATE),
            h_im.reshape(nseq, S5_GROUPS, S5_STATE), k_seq.reshape(nseq, lnew, SB_HEADS, SB_DIM),
            v_seq.reshape(nseq, lnew, SB_HEADS, SB_DIM), conv)


def kernel(x_prompt, x_sample, cache_mla, state_ret, state_s5_re, state_s5_im, cache_sb_k, cache_sb_v,
           state_ffn_conv, page_table, norm_gains, w_in_l0, mla_q_norm, mla_w_uq, mla_kv_norm, mla_w_uk,
           mla_w_uv, w_out_l0, w_in_l1, s5_lam_re, s5_lam_im, s5_log_dt, s5_b_re, s5_b_im, s5_c_re, s5_c_im,
           s5_d, s5_w_glu, w_out_l1, ffn_w_up, ffn_conv_w, ffn_conv_b, ffn_w_down):
    w0 = _prep_layer0(w_in_l0, mla_q_norm, mla_w_uq, mla_kv_norm, mla_w_uk, mla_w_uv, w_out_l0)
    w1 = dict(w_in1=w_in_l1.astype(BF16), w_out_a=w_out_l1[:S5_CH].astype(BF16),
              w_out_b=w_out_l1[S5_CH:].astype(BF16))
    ws5 = _prep_s5(s5_lam_re, s5_lam_im, s5_log_dt, s5_b_re, s5_b_im, s5_c_re, s5_c_im, s5_d, s5_w_glu)
    ffn = dict(w_up=ffn_w_up.astype(BF16), conv_w=ffn_conv_w, conv_b=ffn_conv_b, w_down=ffn_w_down.astype(BF16))
    gains = norm_gains.astype(F32)
    (y_p, mla_p, ret_p, s5r_p, s5i_p, sbk_p, sbv_p, conv_p) = _run_prompt(x_prompt, gains, w0, w1, ws5, ffn)
    (y_s, mla_s, ret_s, s5r_s, s5i_s, sbk_s, sbv_s, conv_s) = _run_sample(
        x_sample, gains, w0, w1, ws5, ffn, cache_mla, state_ret, state_s5_re, state_s5_im, cache_sb_k, cache_sb_v,
        state_ffn_conv, page_table)
    return (y_p, y_s, mla_p, mla_s, ret_p, ret_s, s5r_p, s5i_p, s5r_s, s5i_s, sbk_p, sbv_p, sbk_s, sbv_s,
            conv_p, conv_s)
```

```python
import functools
import math

import jax
import jax.numpy as jnp
import numpy as np
from jax import lax
from jax.experimental import pallas as pl
from jax.experimental.pallas import tpu as pltpu

F32 = jnp.float32
BF16 = jnp.bfloat16

D_MODEL = 1024
PAGE_SIZE = 128
MLA_HEADS = 8
MLA_Q_RANK = 384
MLA_KV_RANK = 256
MLA_NOPE = 64
MLA_ROPE = 32
MLA_V = 64
MLA_ROW = MLA_KV_RANK + MLA_ROPE
MLA_W = MLA_HEADS * MLA_V
RET_HEADS = 4
RET_DK = 64
RET_DV = 128
RET_W = RET_HEADS * RET_DV
RET_QK = RET_HEADS * RET_DK
RET_CHUNK = 128
S5_GROUP = 16
S5_CH = 640
S5_GROUPS = S5_CH // S5_GROUP
S5_STATE = 64
S5_HID = S5_GROUPS * S5_STATE
SB_HEADS = 6
SB_DIM = 64
SB_W = SB_HEADS * SB_DIM
D_FF = 2816
CONV_W = 3
ROPE_THETA = 10000.0
EPS = 1e-6
NEG = -1e30

VMEM_LIMIT_BYTES = 56 * 1024 * 1024
LANES = 128


def _cparams(sem):
    return pltpu.CompilerParams(dimension_semantics=sem, vmem_limit_bytes=VMEM_LIMIT_BYTES)


def _const_spec(shape):
    nd = len(shape)
    return pl.BlockSpec(shape, lambda *_: (0,) * nd, pipeline_mode=pl.Buffered(1))


def _rms(x):
    return x * lax.rsqrt(jnp.mean(x * x, axis=-1, keepdims=True) + EPS)


def _gelu_tanh(x):
    c = math.sqrt(2.0 / math.pi)
    return x * (0.5 * (1.0 + jnp.tanh(c * (x + 0.044715 * (x * x * x)))))


def _dot(a, b):
    return jnp.dot(a, b, preferred_element_type=F32)


def _dot_nt(a, b):
    return lax.dot_general(a, b, (((1,), (1,)), ((), ())), preferred_element_type=F32)


def _lane_band(x, lo, width):
    lane = lax.broadcasted_iota(jnp.int32, x.shape, x.ndim - 1)
    return jnp.where((lane >= lo) & (lane < lo + width), x, jnp.zeros_like(x))


_O_CQ, _O_CKV, _O_KPE, _O_KPES = 0, 384, 640, 896
_O_QR, _O_QRS, _O_KR, _O_KRS, _O_VR, _O_GR = 1152, 1408, 1664, 1920, 2176, 2688
IN0_AUG = 3200
_MLA_SCALE = (MLA_NOPE + MLA_ROPE) ** -0.5


def _inproj0_kernel(x_ref, g_ref, win_ref, qn_ref, wuq_ref, kvn_ref, wuk_ref, cm_ref, sm_ref, cr_ref, sr_ref,
                    qlat_ref, qpe_ref, kfull_ref, rows_ref, qr_ref, kr_ref, vr_ref, sg_ref):
    xn = (_rms(x_ref[...]) * g_ref[...]).astype(BF16)
    z = _dot(xn, win_ref[...])
    cqn = (_rms(z[:, _O_CQ:_O_CQ + MLA_Q_RANK]) * qn_ref[...]).astype(BF16)
    q = _dot(cqn, wuq_ref[...])
    for p in range(MLA_HEADS // 2):
        ql = _dot(q[:, 128 * p:128 * (p + 1)].astype(BF16), wuk_ref[p]) * _MLA_SCALE
        qlat_ref[2 * p] = ql[:, :MLA_KV_RANK].astype(qlat_ref.dtype)
        qlat_ref[2 * p + 1] = ql[:, MLA_KV_RANK:].astype(qlat_ref.dtype)
    cm, sm = cm_ref[...], sm_ref[...]
    qpe = (q[:, 512:768] * cm + q[:, 768:1024] * sm) * _MLA_SCALE
    qpe_ref[...] = qpe.astype(qpe_ref.dtype)
    ckv = _rms(z[:, _O_CKV:_O_CKV + MLA_KV_RANK]) * kvn_ref[...]
    kpe = z[:, _O_KPE:_O_KPE + 256] * cm + z[:, _O_KPES:_O_KPES + 256] * sm
    kfull_ref[:, 0:256] = ckv.astype(kfull_ref.dtype)
    kfull_ref[:, 256:512] = kpe.astype(kfull_ref.dtype)
    rows_ref[:, 0:MLA_KV_RANK] = ckv
    rows_ref[:, MLA_KV_RANK:MLA_ROW] = kpe[:, 0:MLA_ROPE]
    cr, sr = cr_ref[...], sr_ref[...]
    qr_ref[...] = z[:, _O_QR:_O_QR + RET_QK] * cr + z[:, _O_QRS:_O_QRS + RET_QK] * sr
    kr_ref[...] = (z[:, _O_KR:_O_KR + RET_QK] * cr + z[:, _O_KRS:_O_KRS + RET_QK] * sr) * (RET_DK ** -0.5)
    vr_ref[...] = z[:, _O_VR:_O_VR + RET_W].astype(vr_ref.dtype)
    g = z[:, _O_GR:_O_GR + RET_W]
    sg_ref[...] = g * jax.nn.sigmoid(g)


def _inproj0(x2d, gain, w, tabs, *, nseq, seqlen, tm, idt):
    nt = seqlen // tm
    cm, sm, cr, sr = tabs
    row_in = lambda b, i: (b * nt + i, 0)
    tm_out = lambda b, i: (i, b)
    tab = lambda b, i: (i, 0)
    out_shape = (
        jax.ShapeDtypeStruct((MLA_HEADS, seqlen, nseq * 256), idt),
        jax.ShapeDtypeStruct((seqlen, nseq * 256), idt),
        jax.ShapeDtypeStruct((seqlen, nseq * 512), idt),
        jax.ShapeDtypeStruct((nseq * seqlen, MLA_ROW), F32),
        jax.ShapeDtypeStruct((seqlen, nseq * RET_QK), F32),
        jax.ShapeDtypeStruct((seqlen, nseq * RET_QK), F32),
        jax.ShapeDtypeStruct((seqlen, nseq * RET_W), idt),
        jax.ShapeDtypeStruct((seqlen, nseq * RET_W), F32),
    )
    out_specs = (
        pl.BlockSpec((MLA_HEADS, tm, 256), lambda b, i: (0, i, b)),
        pl.BlockSpec((tm, 256), tm_out),
        pl.BlockSpec((tm, 512), tm_out),
        pl.BlockSpec((tm, MLA_ROW), row_in),
        pl.BlockSpec((tm, RET_QK), tm_out),
        pl.BlockSpec((tm, RET_QK), tm_out),
        pl.BlockSpec((tm, RET_W), tm_out),
        pl.BlockSpec((tm, RET_W), tm_out),
    )
    in_specs = [
        pl.BlockSpec((tm, D_MODEL), row_in),
        _const_spec((1, D_MODEL)),
        _const_spec((D_MODEL, IN0_AUG)),
        _const_spec((1, MLA_Q_RANK)),
        _const_spec((MLA_Q_RANK, 1024)),
        _const_spec((1, MLA_KV_RANK)),
        _const_spec((MLA_HEADS // 2, 128, 512)),
        pl.BlockSpec((tm, 256), tab), pl.BlockSpec((tm, 256), tab),
        pl.BlockSpec((tm, 256), tab), pl.BlockSpec((tm, 256), tab),
    ]
    return pl.pallas_call(
        _inproj0_kernel, grid=(nseq, nt), in_specs=in_specs, out_specs=out_specs, out_shape=out_shape,
        compiler_params=_cparams(("parallel", "parallel")), name="inproj0",
    )(x2d, gain, w["w_in0"], w["q_norm"], w["w_uq"], w["kv_norm"], w["w_uk"], cm, sm, cr, sr)


def _mla_prompt_kernel(qlat_ref, qpe_ref, k_ref, wuv_ref, o_ref, qs_ref, s_ref, mx_ref, ls_ref, acc_ref, *, tq):
    qi = pl.program_id(1)
    rows = MLA_HEADS * tq
    half = tq // 2
    assert half == LANES
    qpe = qpe_ref[...]
    for h in range(MLA_HEADS):
        qs_ref[h * tq:(h + 1) * tq, 0:256] = qlat_ref[h]
        qs_ref[h * tq:(h + 1) * tq, 256:512] = _lane_band(qpe, MLA_ROPE * h, MLA_ROPE)
    mx_ref[...] = jnp.full((rows, LANES), NEG, F32)
    ls_ref[...] = jnp.zeros((rows, LANES), F32)
    acc_ref[...] = jnp.zeros((rows, MLA_KV_RANK), F32)

    def kblock(j):
        return k_ref[pl.ds(pl.multiple_of(j * tq, tq), tq), :]

    def pass1(j, c):
        s = _dot_nt(qs_ref[...], kblock(j))
        s_ref[j] = s
        mx_ref[...] = jnp.maximum(mx_ref[...], jnp.maximum(s[:, :half], s[:, half:]))
        return c

    lax.fori_loop(0, qi, pass1, 0)
    t_q = lax.broadcasted_iota(jnp.int32, (rows, tq), 0) & (tq - 1)
    t_k = lax.broadcasted_iota(jnp.int32, (rows, tq), 1)
    s = jnp.where(t_k <= t_q, _dot_nt(qs_ref[...], kblock(qi)), NEG)
    s_ref[qi] = s
    mx = jnp.maximum(mx_ref[...], jnp.maximum(s[:, :half], s[:, half:]))
    mx_ref[...] = jnp.broadcast_to(jnp.max(mx, axis=-1, keepdims=True), (rows, LANES))

    def pass2(j, c):
        s = s_ref[j]
        m = mx_ref[...]
        p0 = jnp.exp(s[:, :half] - m)
        p1 = jnp.exp(s[:, half:] - m)
        ls_ref[...] += p0 + p1
        p = jnp.concatenate([p0, p1], axis=1).astype(BF16)
        acc_ref[...] += _dot(p, kblock(j)[:, 0:MLA_KV_RANK])
        return c

    lax.fori_loop(0, qi + 1, pass2, 0)
    l = jnp.sum(ls_ref[...], axis=-1, keepdims=True)
    o_lat = (acc_ref[...] / l).astype(BF16)
    out = _dot(o_lat[0:tq], wuv_ref[0])
    for h in range(1, MLA_HEADS):
        out = out + _dot(o_lat[h * tq:(h + 1) * tq], wuv_ref[h])
    o_ref[...] = out.astype(o_ref.dtype)


def _mla_prompt(qlat, qpe, kfull, wuv, *, nseq, seqlen, tq):
    nq = seqlen // tq
    rows = MLA_HEADS * tq
    assert tq & (tq - 1) == 0
    return pl.pallas_call(
        functools.partial(_mla_prompt_kernel, tq=tq),
        grid=(nseq, nq),
        in_specs=[
            pl.BlockSpec((MLA_HEADS, tq, 256), lambda b, i: (0, i, b)),
            pl.BlockSpec((tq, 256), lambda b, i: (i, b)),
            pl.BlockSpec((seqlen, 512), lambda b, i: (0, b)),
            _const_spec((MLA_HEADS, MLA_KV_RANK, MLA_W)),
        ],
        out_specs=pl.BlockSpec((tq, MLA_W), lambda b, i: (i, b)),
        out_shape=jax.ShapeDtypeStruct((seqlen, nseq * MLA_W), BF16),
        scratch_shapes=[pltpu.VMEM((rows, 512), BF16), pltpu.VMEM((nq, rows, tq), F32),
                        pltpu.VMEM((rows, LANES), F32), pltpu.VMEM((rows, LANES), F32),
                        pltpu.VMEM((rows, MLA_KV_RANK), F32)],
        compiler_params=_cparams(("parallel", "parallel")), name="mla_prompt",
    )(qlat, qpe, kfull, wuv)


def _mla_sample_kernel(pt_ref, qs_ref, knt_ref, wuv_ref, *refs, npage, nstep, lnew):
    pages = refs[:npage]
    o_ref = refs[npage]
    m_ref, l_ref, acc_ref = refs[npage + 1:]
    step = pl.program_id(1)
    rows = MLA_HEADS * lnew

    @pl.when(step == 0)
    def _():
        m_ref[...] = jnp.full((rows, 1), NEG, F32)
        l_ref[...] = jnp.zeros((rows, 1), F32)
        acc_ref[...] = jnp.zeros((rows, MLA_KV_RANK), F32)

    q = qs_ref[0].astype(BF16)

    def update(s, kts):
        m_prev = m_ref[...]
        m_new = jnp.maximum(m_prev, jnp.max(s, axis=-1, keepdims=True))
        alpha = jnp.exp(m_prev - m_new)
        p = jnp.exp(s - m_new)
        l_ref[...] = alpha * l_ref[...] + jnp.sum(p, axis=-1, keepdims=True)
        pv = _dot_nt(p[:, 0:PAGE_SIZE].astype(BF16), kts[0][0:MLA_KV_RANK])
        for n in range(1, len(kts)):
            pv = pv + _dot_nt(p[:, n * PAGE_SIZE:(n + 1) * PAGE_SIZE].astype(BF16), kts[n][0:MLA_KV_RANK])
        acc_ref[...] = alpha * acc_ref[...] + pv
        m_ref[...] = m_new

    kts = [pages[n][0].astype(BF16) for n in range(npage)]
    update(jnp.concatenate([_dot(q, kt) for kt in kts], axis=1), kts)

    @pl.when(step == nstep - 1)
    def _():
        knt = knt_ref[0].astype(BF16)
        sn = _dot(q, knt)
        t_q = lax.broadcasted_iota(jnp.int32, (rows, PAGE_SIZE), 0) & (lnew - 1)
        t_k = lax.broadcasted_iota(jnp.int32, (rows, PAGE_SIZE), 1)
        update(jnp.where(t_k <= t_q, sn, NEG), [knt])
        o_lat = (acc_ref[...] / l_ref[...]).astype(BF16)
        out = _dot(o_lat[0:lnew], wuv_ref[0])
        for h in range(1, MLA_HEADS):
            out = out + _dot(o_lat[h * lnew:(h + 1) * lnew], wuv_ref[h])
        o_ref[0] = out


def _mla_sample(page_table, qs, knew_t, wuv, cache_t, *, npage):
    nseq, n_pages = page_table.shape
    rows = qs.shape[1]
    lnew = rows // MLA_HEADS
    assert lnew & (lnew - 1) == 0 and lnew <= PAGE_SIZE
    nstep = n_pages // npage
    page_specs = [
        pl.BlockSpec((1, MLA_ROW, PAGE_SIZE), lambda b, s, pt, n=n: (pt[b, s * npage + n], 0, 0))
        for n in range(npage)
    ]
    grid_spec = pltpu.PrefetchScalarGridSpec(
        num_scalar_prefetch=1, grid=(nseq, nstep),
        in_specs=[
            pl.BlockSpec((1, rows, MLA_ROW), lambda b, s, pt: (b, 0, 0)),
            pl.BlockSpec((1, MLA_ROW, PAGE_SIZE), lambda b, s, pt: (b, 0, 0)),
            pl.BlockSpec((MLA_HEADS, MLA_KV_RANK, MLA_W), lambda b, s, pt: (0, 0, 0)),
        ] + page_specs,
        out_specs=pl.BlockSpec((1, lnew, MLA_W), lambda b, s, pt: (b, 0, 0)),
        scratch_shapes=[pltpu.VMEM((rows, 1), F32), pltpu.VMEM((rows, 1), F32),
                        pltpu.VMEM((rows, MLA_KV_RANK), F32)],
    )
    return pl.pallas_call(
        functools.partial(_mla_sample_kernel, npage=npage, nstep=nstep, lnew=lnew),
        grid_spec=grid_spec,
        out_shape=jax.ShapeDtypeStruct((nseq, lnew, MLA_W), F32),
        compiler_params=_cparams(("parallel", "arbitrary")), name="mla_sample",
    )(page_table, qs, knew_t, wuv, *([cache_t] * npage))


def _retention_kernel(q_ref, k_ref, v_ref, sg_ref, intra_ref, qdec_ref, kdec_ref, cdec_ref, s0_ref,
                      o_ref, so_ref, s_ref, pad_ref, *, c, nchunk, lrows):
    i = pl.program_id(1)

    @pl.when(i == 0)
    def _():
        s_ref[...] = s0_ref[0]

    row = lax.broadcasted_iota(jnp.int32, (RET_QK, RET_DV), 0)
    for ch in range(nchunk):
        if lrows < c:
            pad_ref[...] = jnp.zeros(pad_ref.shape, F32)
            pad_ref[0:lrows, 0:RET_QK] = q_ref[...]
            pad_ref[0:lrows, RET_QK:2 * RET_QK] = k_ref[...]
            pad_ref[0:lrows, 2 * RET_QK:2 * RET_QK + RET_W] = v_ref[...].astype(F32)
            q = pad_ref[:, 0:RET_QK]
            k = pad_ref[:, RET_QK:2 * RET_QK]
            v_all = pad_ref[:, 2 * RET_QK:2 * RET_QK + RET_W].astype(BF16)
        else:
            rs = slice(ch * c, (ch + 1) * c)
            q, k, v_all = q_ref[rs, :], k_ref[rs, :], v_ref[rs, :].astype(BF16)
        qd = q * qdec_ref[...]
        kdt = (k * kdec_ref[...]).T.astype(BF16)
        kb = k.astype(BF16)
        s_old = s_ref[...]
        sb = s_old.astype(BF16)
        s_new = cdec_ref[...] * s_old
        for h in range(RET_HEADS):
            qh = _lane_band(q, RET_DK * h, RET_DK).astype(BF16)
            qdh = _lane_band(qd, RET_DK * h, RET_DK).astype(BF16)
            vh = v_all[:, RET_DV * h:RET_DV * (h + 1)]
            att = _dot_nt(qh, kb) * intra_ref[h]
            o = _dot(att.astype(BF16), vh) + _dot(qdh, sb)
            o = _rms(o)
            if lrows < c:
                o_ref[:, RET_DV * h:RET_DV * (h + 1)] = (
                    o[0:lrows] * sg_ref[:, RET_DV * h:RET_DV * (h + 1)]).astype(o_ref.dtype)
            else:
                o_ref[rs, RET_DV * h:RET_DV * (h + 1)] = (
                    o * sg_ref[rs, RET_DV * h:RET_DV * (h + 1)]).astype(o_ref.dtype)
            upd = _dot(kdt, vh)
            s_new = s_new + jnp.where((row >= RET_DK * h) & (row < RET_DK * (h + 1)), upd, 0.0)
        s_ref[...] = s_new

    @pl.when(i == pl.num_programs(1) - 1)
    def _():
        so_ref[0] = s_ref[...]


def _retention_tables(c_eff, c_pad):
    log_g = jnp.log(1.0 - 2.0 ** (-5.0 - jnp.arange(RET_HEADS, dtype=F32)))
    i = jnp.arange(c_pad, dtype=F32)
    diff = i[:, None] - i[None, :]
    intra = jnp.where(diff >= 0, jnp.exp(jnp.maximum(diff, 0.0)[None] * log_g[:, None, None]), 0.0)
    q_dec = jnp.exp((i[:, None] + 1.0) * log_g[None, :])
    k_dec = jnp.exp(jnp.maximum(c_eff - 1.0 - i, 0.0)[:, None] * log_g[None, :])
    c_dec = jnp.exp(c_eff * log_g)
    rep = lambda a: jnp.repeat(a, RET_DK, axis=1)
    cdec_rows = jnp.broadcast_to(jnp.repeat(c_dec, RET_DK)[:, None], (RET_QK, RET_DV))
    return intra.astype(F32), rep(q_dec), rep(k_dec), cdec_rows


def _retention(qr, kr, vr, sg, s0, *, nseq, seqlen, odt):
    c = RET_CHUNK
    if seqlen % c == 0:
        c_eff, tr = c, _tile(seqlen, (8 * c, 4 * c, 2 * c, c))
        nchunk, lrows = tr // c, tr
    else:
        c_eff, tr, nchunk, lrows = seqlen, seqlen, 1, seqlen
    nt = seqlen // tr
    intra, qdec, kdec, cdec = _retention_tables(c_eff, c)
    blk = lambda w: pl.BlockSpec((tr, w), lambda b, i: (i, b))
    return pl.pallas_call(
        functools.partial(_retention_kernel, c=c, nchunk=nchunk, lrows=lrows),
        grid=(nseq, nt),
        in_specs=[blk(RET_QK), blk(RET_QK), blk(RET_W), blk(RET_W),
                  _const_spec((RET_HEADS, c, c)), _const_spec((c, RET_QK)), _const_spec((c, RET_QK)),
                  _const_spec((RET_QK, RET_DV)),
                  pl.BlockSpec((1, RET_QK, RET_DV), lambda b, i: (b, 0, 0))],
        out_specs=(blk(RET_W), pl.BlockSpec((1, RET_QK, RET_DV), lambda b, i: (b, 0, 0))),
        out_shape=(jax.ShapeDtypeStruct((seqlen, nseq * RET_W), odt),
                   jax.ShapeDtypeStruct((nseq, RET_QK, RET_DV), F32)),
        scratch_shapes=[pltpu.VMEM((RET_QK, RET_DV), F32), pltpu.VMEM((c, 2 * RET_QK + RET_W), F32)],
        compiler_params=_cparams(("parallel", "arbitrary")), name="retention",
    )(qr, kr, vr, sg, intra, qdec, kdec, cdec, s0)


def _outproj_kernel(x_ref, a_ref, b_ref, wa_ref, wb_ref, g_ref, y_ref):
    m = _dot(a_ref[...].astype(BF16), wa_ref[...]) + _dot(b_ref[...].astype(BF16), wb_ref[...])
    y_ref[...] = x_ref[...] + _rms(m) * g_ref[...]


def _outproj(x, a, b, wa, wb, gain, *, grid, x_map, ab_map, y_map, y_shape, tm):
    wa_w, wb_w = wa.shape[0], wb.shape[0]
    return pl.pallas_call(
        _outproj_kernel, grid=grid,
        in_specs=[pl.BlockSpec((tm, D_MODEL), x_map), pl.BlockSpec((tm, wa_w), ab_map),
                  pl.BlockSpec((tm, wb_w), ab_map), _const_spec(wa.shape), _const_spec(wb.shape),
                  _const_spec((1, D_MODEL))],
        out_specs=pl.BlockSpec((tm, D_MODEL), y_map),
        out_shape=jax.ShapeDtypeStruct(y_shape, F32),
        compiler_params=_cparams(("parallel",) * len(grid)), name="outproj",
    )(x, a, b, wa, wb, gain)


_FFN_CK = 256


def _ffn_kernel(*refs, nb, tm, mixer):
    if mixer:
        x_ref, a_ref, b_ref, wa_ref, wb_ref, g1_ref = refs[:6]
        refs = refs[6:]
    else:
        x_ref = refs[0]
        refs = refs[1:]
    g2_ref, wup_ref, cw_ref, cb_ref, wdn_ref, g3_ref, st_ref, y_ref, cs_ref, carry_ref = refs
    i = pl.program_id(0)
    ck = _FFN_CK

    @pl.when(i == 0)
    def _():
        carry_ref[...] = st_ref[...]

    x = x_ref[...]
    if mixer:
        m = _dot(a_ref[...].astype(BF16), wa_ref[...]) + _dot(b_ref[...].astype(BF16), wb_ref[...])
        x = x + _rms(m) * g1_ref[...]
    xn = (_rms(x) * g2_ref[...]).astype(BF16)
    acc = jnp.zeros((tm, D_MODEL), F32)
    for c in range(D_FF // ck):
        act = []
        for half in range(2):
            col = half * D_FF + c * ck
            cs = slice(col, col + ck)
            h = _dot(xn, wup_ref[:, cs])
            prev2, prev1 = carry_ref[0:nb, cs], carry_ref[nb:2 * nb, cs]
            carry_ref[:, cs] = h[tm - 2 * nb:tm]
            hm2 = jnp.concatenate([prev2, prev1, h[0:tm - 2 * nb]], axis=0)
            hm1 = jnp.concatenate([prev1, h[0:tm - nb]], axis=0)
            act.append(cb_ref[:, cs] + cw_ref[0:1, cs] * hm2 + cw_ref[1:2, cs] * hm1 + cw_ref[2:3, cs] * h)
        a = (_gelu_tanh(act[0]) * act[1]).astype(BF16)
        acc = acc + _dot(a, wdn_ref[c * ck:(c + 1) * ck, :])
    y_ref[...] = x + _rms(acc) * g3_ref[...]

    @pl.when(i == pl.num_programs(0) - 1)
    def _():
        cs_ref[...] = carry_ref[...]


def _ffn(x, g2, wup, cw, cb, wdn, g3, state, *, nb, tm, mixer=None):
    rows = x.shape[0]
    assert tm >= 2 * nb and tm % nb == 0 and rows % tm == 0
    blk = lambda w: pl.BlockSpec((tm, w), lambda i: (i, 0))
    ins, specs = [x], [blk(D_MODEL)]
    if mixer is not None:
        a, b, wa, wb, g1 = mixer
        ins += [a, b, wa, wb, g1]
        specs += [blk(a.shape[1]), blk(b.shape[1]), _const_spec(wa.shape), _const_spec(wb.shape),
                  _const_spec((1, D_MODEL))]
    ins += [g2, wup, cw, cb, wdn, g3, state]
    specs += [_const_spec((1, D_MODEL)), _const_spec((D_MODEL, 2 * D_FF)), _const_spec((CONV_W, 2 * D_FF)),
              _const_spec((1, 2 * D_FF)), _const_spec((D_FF, D_MODEL)), _const_spec((1, D_MODEL)),
              _const_spec((2 * nb, 2 * D_FF))]
    return pl.pallas_call(
        functools.partial(_ffn_kernel, nb=nb, tm=tm, mixer=mixer is not None),
        grid=(rows // tm,),
        in_specs=specs,
        out_specs=(blk(D_MODEL), pl.BlockSpec((2 * nb, 2 * D_FF), lambda i: (0, 0))),
        out_shape=(jax.ShapeDtypeStruct((rows, D_MODEL), F32), jax.ShapeDtypeStruct((2 * nb, 2 * D_FF), F32)),
        scratch_shapes=[pltpu.VMEM((2 * nb, 2 * D_FF), F32)],
        compiler_params=_cparams(("arbitrary",)), name="convffn",
    )(*ins)


def _inproj1_kernel(x_ref, g_ref, win_ref, u_ref, q_ref, k_ref, v_ref, kb_ref, vb_ref):
    xn = (_rms(x_ref[...]) * g_ref[...]).astype(BF16)
    z = _dot(xn, win_ref[...])
    u_ref[...] = z[:, 0:S5_CH]
    q_ref[...] = (z[:, S5_CH:S5_CH + SB_W] * (SB_DIM ** -0.5)).astype(q_ref.dtype)
    k = z[:, S5_CH + SB_W:S5_CH + 2 * SB_W]
    v = z[:, S5_CH + 2 * SB_W:S5_CH + 3 * SB_W]
    k_ref[...] = k
    v_ref[...] = v
    kb_ref[...] = k.astype(kb_ref.dtype)
    vb_ref[...] = v.astype(vb_ref.dtype)


def _inproj1(x, gain, win, *, tm, idt):
    rows = x.shape[0]
    blk = lambda w: pl.BlockSpec((tm, w), lambda i: (i, 0))
    sds = lambda w, dt: jax.ShapeDtypeStruct((rows, w), dt)
    return pl.pallas_call(
        _inproj1_kernel, grid=(rows // tm,),
        in_specs=[blk(D_MODEL), _const_spec((1, D_MODEL)), _const_spec(win.shape)],
        out_specs=(blk(S5_CH), blk(SB_W), blk(SB_W), blk(SB_W), blk(SB_W), blk(SB_W)),
        out_shape=(sds(S5_CH, F32), sds(SB_W, idt), sds(SB_W, F32), sds(SB_W, F32), sds(SB_W, idt), sds(SB_W, idt)),
        compiler_params=_cparams(("parallel",)), name="inproj1",
    )(x, gain, win)


_S5_LW = 256
_S5_CB = 128
_S5_NBLK = S5_CH // _S5_CB
_S5_HB = S5_HID // _S5_NBLK


def _s5_kernel(u_ref, h0r_ref, h0i_ref, br_ref, bi_ref, ar_ref, ai_ref, cr_ref, cin_ref, d_ref, wg_ref,
               o_ref, hr_ref, hi_ref, *, tc, bb):
    i = pl.program_id(1)

    @pl.when(i == 0)
    def _():
        hr_ref[...] = h0r_ref[...]
        hi_ref[...] = h0i_ref[...]

    u = u_ref[...].reshape(tc * bb, S5_CH)
    ub = u.astype(BF16)
    ys = []
    for m in range(_S5_NBLK):
        um = ub[:, _S5_CB * m:_S5_CB * (m + 1)]
        sr, si = _dot(um, br_ref[m]), _dot(um, bi_ref[m])
        hre, him = [], []
        for sub in range(_S5_HB // _S5_LW):
            ls = slice(sub * _S5_LW, (sub + 1) * _S5_LW)
            cs = slice(_S5_HB * m + sub * _S5_LW, _S5_HB * m + (sub + 1) * _S5_LW)
            ar, ai = ar_ref[:, cs], ai_ref[:, cs]
            hr, hi = hr_ref[:, cs], hi_ref[:, cs]
            hrs, his = [], []
            for t in range(tc):
                rs = slice(t * bb, (t + 1) * bb)
                hr, hi = ar * hr - ai * hi + sr[rs, ls], ar * hi + ai * hr + si[rs, ls]
                hrs.append(hr)
                his.append(hi)
            hr_ref[:, cs] = hr
            hi_ref[:, cs] = hi
            hre.append(jnp.concatenate(hrs, axis=0))
            him.append(jnp.concatenate(his, axis=0))
        ys.append(_dot(jnp.concatenate(hre, axis=1).astype(BF16), cr_ref[m])
                  + _dot(jnp.concatenate(him, axis=1).astype(BF16), cin_ref[m]))
    y = jnp.concatenate(ys, axis=1)
    y = _gelu_tanh(y + d_ref[...] * u)
    o = y * jax.nn.sigmoid(_dot(y.astype(BF16), wg_ref[...]))
    o_ref[...] = o.reshape(tc, bb, S5_CH).astype(o_ref.dtype)


def _s5(u3, h0r, h0i, w, *, tc, bb, odt):
    tlen, nseq, _ = u3.shape
    ublk = pl.BlockSpec((tc, bb, S5_CH), lambda j, i: (i, j, 0))
    hblk = pl.BlockSpec((bb, S5_HID), lambda j, i: (j, 0))
    return pl.pallas_call(
        functools.partial(_s5_kernel, tc=tc, bb=bb),
        grid=(nseq // bb, tlen // tc),
        in_specs=[ublk, hblk, hblk, _const_spec((_S5_NBLK, _S5_CB, _S5_HB)), _const_spec((_S5_NBLK, _S5_CB, _S5_HB)),
                  _const_spec((1, S5_HID)), _const_spec((1, S5_HID)), _const_spec((_S5_NBLK, _S5_HB, _S5_CB)),
                  _const_spec((_S5_NBLK, _S5_HB, _S5_CB)), _const_spec((1, S5_CH)), _const_spec((S5_CH, S5_CH))],
        out_specs=(ublk, hblk, hblk),
        out_shape=(jax.ShapeDtypeStruct((tlen, nseq, S5_CH), odt), jax.ShapeDtypeStruct((nseq, S5_HID), F32),
                   jax.ShapeDtypeStruct((nseq, S5_HID), F32)),
        compiler_params=_cparams(("parallel", "arbitrary")), name="s5",
    )(u3, h0r, h0i, w["bbr"], w["bbi"], w["abr"], w["abi"], w["ccr"], w["ccin"], w["d"], w["w_glu"])


def _sb_weights(z, tri, carry, valid):
    t = jnp.log(1.0 + jnp.exp(-jnp.abs(z)))
    lsz = jnp.minimum(z, 0.0) - t
    lk = lsz - z
    if valid is not None:
        lk = jnp.where(valid, lk, 0.0)
    hi = lk.astype(BF16)
    lo = (lk - hi.astype(F32)).astype(BF16)
    later = _dot(hi, tri) + _dot(lo, tri) + carry
    w = jnp.exp(lsz + later)
    if valid is not None:
        w = jnp.where(valid, w, 0.0)
    return w.astype(BF16), jnp.sum(lk, axis=-1, keepdims=True)


def _sb_prompt_kernel(q_ref, kb_ref, vb_ref, k32_ref, v32_ref, tri_ref, o_ref, ko_ref, vo_ref,
                      qm_ref, carry_ref, acc_ref, *, tq):
    qi = pl.program_id(1)
    rows = SB_HEADS * tq
    ko_ref[...] = k32_ref[...]
    vo_ref[...] = v32_ref[...]
    tri = tri_ref[...]
    for h in range(SB_HEADS):
        qm_ref[h] = _lane_band(q_ref[:, LANES * (h // 2):LANES * (h // 2 + 1)], SB_DIM * (h % 2), SB_DIM)
    carry_ref[...] = jnp.zeros(carry_ref.shape, F32)
    acc_ref[...] = jnp.zeros(acc_ref.shape, F32)

    def visit(j, valid):
        rs = pl.ds(pl.multiple_of(j * tq, tq), tq)
        ps = [slice(LANES * (h // 2), LANES * (h // 2 + 1)) for h in range(SB_HEADS)]
        z = jnp.concatenate([_dot_nt(qm_ref[h], kb_ref[rs, ps[h]]) for h in range(SB_HEADS)], axis=0)
        w, lksum = _sb_weights(z, tri, carry_ref[...], valid)
        for h in range(SB_HEADS):
            acc_ref[h] += _dot(w[h * tq:(h + 1) * tq], vb_ref[rs, ps[h]])
        carry_ref[...] += lksum

    t_q = lax.broadcasted_iota(jnp.int32, (rows, tq), 0) & (tq - 1)
    t_k = lax.broadcasted_iota(jnp.int32, (rows, tq), 1)
    visit(qi, t_k < t_q)

    def body(jj, c):
        visit(qi - 1 - jj, None)
        return c

    lax.fori_loop(0, qi, body, 0)
    lane = lax.broadcasted_iota(jnp.int32, (tq, LANES), 1)
    for p in range(SB_HEADS // 2):
        o_ref[:, LANES * p:LANES * (p + 1)] = jnp.where(
            lane < SB_DIM, acc_ref[2 * p], acc_ref[2 * p + 1]).astype(o_ref.dtype)


def _tri(n):
    j = jnp.arange(n)
    return (j[:, None] > j[None, :]).astype(BF16)


def _sb_prompt(q, kb, vb, k32, v32, *, nseq, seqlen, tq):
    nq = seqlen // tq
    colblk = pl.BlockSpec((tq, SB_W), lambda b, i: (i, b))
    seqblk = pl.BlockSpec((seqlen, SB_W), lambda b, i: (0, b))
    rowblk = pl.BlockSpec((tq, SB_W), lambda b, i: (b * nq + i, 0))
    return pl.pallas_call(
        functools.partial(_sb_prompt_kernel, tq=tq),
        grid=(nseq, nq),
        in_specs=[colblk, seqblk, seqblk, colblk, colblk, _const_spec((tq, tq))],
        out_specs=(colblk, rowblk, rowblk),
        out_shape=(jax.ShapeDtypeStruct((seqlen, nseq * SB_W), BF16),
                   jax.ShapeDtypeStruct((nseq * seqlen, SB_W), F32),
                   jax.ShapeDtypeStruct((nseq * seqlen, SB_W), F32)),
        scratch_shapes=[pltpu.VMEM((SB_HEADS, tq, LANES), BF16), pltpu.VMEM((SB_HEADS * tq, 1), F32),
                        pltpu.VMEM((SB_HEADS, tq, LANES), F32)],
        compiler_params=_cparams(("parallel", "parallel")), name="sb_prompt",
    )(q, kb, vb, k32, v32, _tri(tq))


def _sb_sample_kernel(pt_ref, q_ref, knt_ref, vnt_ref, tri_ref, *refs, npage, nstep, lnew):
    kpages = refs[:npage]
    vpages = refs[npage:2 * npage]
    o_ref = refs[2 * npage]
    carry_ref, acc_ref = refs[2 * npage + 1:]
    step = pl.program_id(1)
    rows = SB_HEADS * lnew
    tri = tri_ref[...]
    q = q_ref[0].astype(BF16)

    @pl.when(step == 0)
    def _():
        t_q = lax.broadcasted_iota(jnp.int32, (rows, PAGE_SIZE), 0) & (lnew - 1)
        t_k = lax.broadcasted_iota(jnp.int32, (rows, PAGE_SIZE), 1)
        w, lksum = _sb_weights(_dot(q, knt_ref[0].astype(BF16)), tri, jnp.zeros((rows, 1), F32), t_k < t_q)
        acc_ref[...] = _dot_nt(w, vnt_ref[0].astype(BF16))
        carry_ref[...] = lksum

    z = jnp.concatenate([_dot(q, kpages[n][0].reshape(SB_W, PAGE_SIZE).astype(BF16)) for n in range(npage)],
                        axis=0)
    t = jnp.log(1.0 + jnp.exp(-jnp.abs(z)))
    lsz = jnp.minimum(z, 0.0) - t
    lk = lsz - z
    hi = lk.astype(BF16)
    lo = (lk - hi.astype(F32)).astype(BF16)
    cum = _dot(hi, tri) + _dot(lo, tri)
    lksum = jnp.sum(lk, axis=-1, keepdims=True)
    carries, carry = [], carry_ref[...]
    for n in range(npage):
        carries.append(carry)
        carry = carry + lksum[n * rows:(n + 1) * rows]
    carry_ref[...] = carry
    w = jnp.exp(lsz + cum + jnp.concatenate(carries, axis=0)).astype(BF16)
    pv = _dot_nt(w[0:rows], vpages[0][0].reshape(SB_W, PAGE_SIZE).astype(BF16))
    for n in range(1, npage):
        pv = pv + _dot_nt(w[n * rows:(n + 1) * rows], vpages[n][0].reshape(SB_W, PAGE_SIZE).astype(BF16))
    acc_ref[...] += pv

    @pl.when(step == nstep - 1)
    def _():
        acc = acc_ref[...]
        out = _lane_band(acc[0:lnew], 0, SB_DIM)
        for h in range(1, SB_HEADS):
            out = out + _lane_band(acc[h * lnew:(h + 1) * lnew], SB_DIM * h, SB_DIM)
        o_ref[0] = out


def _sb_sample(page_table, q_bd, knew_t, vnew_t, cache_kt, cache_vt, *, npage):
    nseq, n_pages = page_table.shape
    rows = q_bd.shape[1]
    lnew = rows // SB_HEADS
    assert lnew & (lnew - 1) == 0 and lnew <= PAGE_SIZE
    nstep = n_pages // npage
    pidx = lambda b, s, pt, n: (pt[b, n_pages - 1 - (s * npage + n)], 0, 0, 0)
    pspecs = [pl.BlockSpec((1, SB_HEADS, SB_DIM, PAGE_SIZE), functools.partial(pidx, n=n)) for n in range(npage)]
    grid_spec = pltpu.PrefetchScalarGridSpec(
        num_scalar_prefetch=1, grid=(nseq, nstep),
        in_specs=[
            pl.BlockSpec((1, rows, SB_W), lambda b, s, pt: (b, 0, 0)),
            pl.BlockSpec((1, SB_W, PAGE_SIZE), lambda b, s, pt: (b, 0, 0)),
            pl.BlockSpec((1, SB_W, PAGE_SIZE), lambda b, s, pt: (b, 0, 0)),
            pl.BlockSpec((PAGE_SIZE, PAGE_SIZE), lambda b, s, pt: (0, 0)),
        ] + pspecs + pspecs,
        out_specs=pl.BlockSpec((1, lnew, SB_W), lambda b, s, pt: (b, 0, 0)),
        scratch_shapes=[pltpu.VMEM((rows, 1), F32), pltpu.VMEM((rows, SB_W), F32)],
    )
    return pl.pallas_call(
        functools.partial(_sb_sample_kernel, npage=npage, nstep=nstep, lnew=lnew),
        grid_spec=grid_spec,
        out_shape=jax.ShapeDtypeStruct((nseq, lnew, SB_W), F32),
        compiler_params=_cparams(("parallel", "arbitrary")), name="sb_sample",
    )(page_table, q_bd, knew_t, vnew_t, _tri(PAGE_SIZE), *([cache_kt] * npage), *([cache_vt] * npage))


def _copy_kernel(x_ref, y_ref):
    y_ref[...] = x_ref[...]


def _to_seq_major(x_tm, *, nseq, seqlen, tm):
    nt = seqlen // tm
    return pl.pallas_call(
        _copy_kernel, grid=(nseq, nt),
        in_specs=[pl.BlockSpec((tm, D_MODEL), lambda b, i: (i, b))],
        out_specs=pl.BlockSpec((tm, D_MODEL), lambda b, i: (b * nt + i, 0)),
        out_shape=jax.ShapeDtypeStruct((nseq * seqlen, D_MODEL), F32),
        compiler_params=_cparams(("parallel", "parallel")), name="to_seq_major",
    )(x_tm)


def _swap_halves(n_heads, dim):
    idx = np.arange(n_heads * dim)
    return (idx // dim) * dim + (idx % dim + dim // 2) % dim


def _rope_tables(pos, half, reps):
    freqs = ROPE_THETA ** (-jnp.arange(half, dtype=F32) / half)
    ang = pos.astype(F32)[:, None] * freqs[None, :]
    c, s = jnp.cos(ang), jnp.sin(ang)
    return (jnp.tile(jnp.concatenate([c, c], axis=-1), (1, reps)),
            jnp.tile(jnp.concatenate([-s, s], axis=-1), (1, reps)))


def _prep_layer0(w_in, q_norm, w_uq, kv_norm, w_uk, w_uv, w_out):
    o_kpe = MLA_Q_RANK + MLA_KV_RANK
    o_qr = o_kpe + MLA_ROPE
    o_kr = o_qr + RET_QK
    o_v = o_kr + RET_QK
    cols = np.concatenate([
        np.arange(0, o_kpe),
        np.tile(o_kpe + np.arange(MLA_ROPE), MLA_HEADS), np.tile(o_kpe + _swap_halves(1, MLA_ROPE), MLA_HEADS),
        o_qr + np.arange(RET_QK), o_qr + _swap_halves(RET_HEADS, RET_DK),
        o_kr + np.arange(RET_QK), o_kr + _swap_halves(RET_HEADS, RET_DK),
        np.arange(o_v, o_v + 2 * RET_W)])
    assert cols.shape[0] == IN0_AUG
    hd = MLA_NOPE + MLA_ROPE
    heads = np.arange(MLA_HEADS)[:, None] * hd
    uq_cols = np.concatenate([
        (heads + np.arange(MLA_NOPE)[None, :]).ravel(),
        (heads + MLA_NOPE + np.arange(MLA_ROPE)[None, :]).ravel(),
        (heads + MLA_NOPE + _swap_halves(1, MLA_ROPE)[None, :]).ravel()])
    wuk_t = jnp.transpose(w_uk, (1, 2, 0))
    z = jnp.zeros((MLA_NOPE, MLA_KV_RANK), F32)
    wuk_pair = jnp.stack([jnp.block([[wuk_t[2 * p], z], [z, wuk_t[2 * p + 1]]]) for p in range(MLA_HEADS // 2)])
    wuv_pad = jnp.zeros((MLA_HEADS, MLA_KV_RANK, MLA_W), F32)
    for h in range(MLA_HEADS):
        wuv_pad = wuv_pad.at[h, :, MLA_V * h:MLA_V * (h + 1)].set(w_uv[:, h, :])
    return dict(w_in0=w_in[:, cols].astype(BF16), q_norm=q_norm[None, :], w_uq=w_uq[:, uq_cols].astype(BF16),
                kv_norm=kv_norm[None, :], w_uk=wuk_pair.astype(BF16), w_uv=wuv_pad.astype(BF16),
                w_out_a=w_out[:MLA_W].astype(BF16), w_out_b=w_out[MLA_W:].astype(BF16))


def _prep_s5(lam_re, lam_im, log_dt, b_re, b_im, c_re, c_im, d_skip, w_glu):
    lr, li = lam_re.astype(F32), lam_im.astype(F32)
    dt = jnp.exp(log_dt.astype(F32))[:, None]
    mag = jnp.exp(lr * dt)
    ab_re, ab_im = mag * jnp.cos(li * dt), mag * jnp.sin(li * dt)
    nr, ni = ab_re - 1.0, ab_im
    den = lr * lr + li * li
    f_re = (nr * lr + ni * li) / den
    f_im = (ni * lr - nr * li) / den
    bb_re = f_re[..., None] * b_re - f_im[..., None] * b_im
    bb_im = f_re[..., None] * b_im + f_im[..., None] * b_re
    gpb = _S5_CB // S5_GROUP
    eye = jnp.eye(gpb, dtype=F32)
    bd_in = lambda bb: jnp.einsum('mgpc,gh->mgchp', bb.reshape(_S5_NBLK, gpb, S5_STATE, S5_GROUP),
                                  eye).reshape(_S5_NBLK, _S5_CB, _S5_HB)
    bd_out = lambda cc: jnp.einsum('mgcp,gh->mgphc', cc.reshape(_S5_NBLK, gpb, S5_GROUP, S5_STATE),
                                   eye).reshape(_S5_NBLK, _S5_HB, _S5_CB)
    return dict(bbr=bd_in(bb_re).astype(BF16), bbi=bd_in(bb_im).astype(BF16),
                abr=ab_re.reshape(1, S5_HID), abi=ab_im.reshape(1, S5_HID),
                ccr=bd_out(c_re).astype(BF16), ccin=bd_out(-c_im).astype(BF16),
                d=d_skip[None, :], w_glu=w_glu.astype(BF16))


def _tile(n, pref):
    for t in pref:
        if n % t == 0:
            return t
    return n


def _layer_tail(x_tm, layer, gains, ffn, conv_state, *, nb, tm, mixer=None):
    if mixer is not None:
        a, b, w = mixer
        mixer = (a, b, w["w_out_a"], w["w_out_b"], gains[layer, 1][None])
    return _ffn(x_tm, gains[layer, 2][None], ffn["w_up"][layer], ffn["conv_w"][layer], ffn["conv_b"][layer][None],
                ffn["w_down"][layer], gains[layer, 3][None], conv_state, nb=nb, tm=tm, mixer=mixer)


def _run_prompt(x, gains, w0, w1, ws5, ffn):
    nseq, seqlen, _ = x.shape
    rows = nseq * seqlen
    x2d = x.reshape(rows, D_MODEL)
    pos = jnp.arange(seqlen, dtype=jnp.int32)
    tabs = _rope_tables(pos, MLA_ROPE // 2, MLA_HEADS) + _rope_tables(pos, RET_DK // 2, RET_HEADS)
    tm = _tile(seqlen, (512, 256, 128))
    tq = _tile(seqlen, (256, 128))
    qlat, qpe, kfull, rows_new, qr, kr, vr, sg = _inproj0(x2d, gains[0, 0][None], w0, tabs, nseq=nseq,
                                                         seqlen=seqlen, tm=tm, idt=BF16)
    o_mla = _mla_prompt(qlat, qpe, kfull, w0["w_uv"], nseq=nseq, seqlen=seqlen, tq=tq)
    o_ret, s_new = _retention(qr, kr, vr, sg, jnp.zeros((nseq, RET_QK, RET_DV), F32), nseq=nseq, seqlen=seqlen,
                              odt=BF16)
    nt = seqlen // tm
    x1 = _outproj(x2d, o_mla, o_ret, w0["w_out_a"], w0["w_out_b"], gains[0, 1][None], grid=(nseq, nt),
                  x_map=lambda b, i: (b * nt + i, 0), ab_map=lambda b, i: (i, b), y_map=lambda b, i: (i, b),
                  y_shape=(seqlen, nseq * D_MODEL), tm=tm).reshape(rows, D_MODEL)
    tmf = _tile(rows, (512, 256, 128, 64))
    zero_state = jnp.zeros((2 * nseq, 2 * D_FF), F32)
    x2, cs0 = _layer_tail(x1, 0, gains, ffn, zero_state, nb=nseq, tm=tmf)
    u, q, k32, v32, kb, vb = _inproj1(x2, gains[1, 0][None], w1["w_in1"], tm=tmf, idt=BF16)
    tc = _tile(seqlen, (16, 8))
    zero_h = jnp.zeros((nseq, S5_HID), F32)
    o_s5, h_re, h_im = _s5(u.reshape(seqlen, nseq, S5_CH), zero_h, zero_h, ws5, tc=tc, bb=nseq, odt=BF16)
    tm2 = lambda a: a.reshape(seqlen, nseq * SB_W)
    o_sb, k_out, v_out = _sb_prompt(tm2(q), tm2(kb), tm2(vb), tm2(k32), tm2(v32), nseq=nseq, seqlen=seqlen, tq=tq)
    x4, cs1 = _layer_tail(x2, 1, gains, ffn, zero_state, nb=nseq, tm=tmf,
                          mixer=(o_s5.reshape(rows, S5_CH), o_sb.reshape(rows, SB_W), w1))
    y = _to_seq_major(x4.reshape(seqlen, nseq * D_MODEL), nseq=nseq, seqlen=seqlen, tm=tm)
    conv = jnp.stack([cs0, cs1]).reshape(2, CONV_W - 1, nseq, 2 * D_FF).transpose(0, 2, 1, 3)
    return (y.reshape(nseq, seqlen, D_MODEL), rows_new.reshape(nseq, seqlen, MLA_ROW),
            s_new.reshape(nseq, RET_HEADS, RET_DK, RET_DV), h_re.reshape(nseq, S5_GROUPS, S5_STATE),
            h_im.reshape(nseq, S5_GROUPS, S5_STATE), k_out.reshape(nseq, seqlen, SB_HEADS, SB_DIM),
            v_out.reshape(nseq, seqlen, SB_HEADS, SB_DIM), conv)


def _run_sample(x, gains, w0, w1, ws5, ffn, cache_mla, state_ret, s5_re, s5_im, cache_k, cache_v, state_conv,
                page_table):
    nseq, lnew, _ = x.shape
    rows = nseq * lnew
    past_len = page_table.shape[1] * PAGE_SIZE
    to_tm = lambda a: jnp.swapaxes(a, 0, 1)
    x_tm = to_tm(x).reshape(rows, D_MODEL)
    pos = past_len + jnp.arange(lnew, dtype=jnp.int32)
    pos_rows = jnp.repeat(pos, nseq)
    tabs = _rope_tables(pos_rows, MLA_ROPE // 2, MLA_HEADS) + _rope_tables(pos_rows, RET_DK // 2, RET_HEADS)
    tm = _tile(rows, (512, 256, 128, 64))
    qlat, qpe, _, rows_new, qr, kr, vr, sg = _inproj0(x_tm, gains[0, 0][None], w0, tabs, nseq=1, seqlen=rows,
                                                     tm=tm, idt=F32)
    ql = qlat.reshape(MLA_HEADS, lnew, nseq, MLA_KV_RANK).transpose(2, 0, 1, 3)
    qp = qpe.reshape(lnew, nseq, MLA_HEADS, MLA_ROPE).transpose(1, 2, 0, 3)
    qs = jnp.concatenate([ql, qp], axis=-1).reshape(nseq, MLA_HEADS * lnew, MLA_ROW)
    rows_seq = to_tm(rows_new.reshape(lnew, nseq, MLA_ROW))
    pad_keys = lambda a: jnp.pad(jnp.swapaxes(a, 1, 2), ((0, 0), (0, 0), (0, PAGE_SIZE - lnew)))
    npage = _tile(page_table.shape[1], (32, 16, 8, 4, 2))
    o_mla = _mla_sample(page_table, qs, pad_keys(rows_seq), w0["w_uv"], jnp.transpose(cache_mla, (0, 2, 1)),
                        npage=npage)
    col = lambda a, w: a.reshape(lnew, nseq * w)
    o_ret, s_new = _retention(col(qr, RET_QK), col(kr, RET_QK), col(vr, RET_W), col(sg, RET_W),
                              state_ret.reshape(nseq, RET_QK, RET_DV), nseq=nseq, seqlen=lnew, odt=F32)
    conv_tm = lambda st: jnp.swapaxes(st, 0, 1).reshape(2 * nseq, 2 * D_FF)
    tmf = max(2 * nseq, _tile(rows, (512, 256, 128)))
    x2, cs0 = _layer_tail(x_tm, 0, gains, ffn, conv_tm(state_conv[0]), nb=nseq, tm=tmf,
                          mixer=(to_tm(o_mla).reshape(rows, MLA_W), o_ret.reshape(rows, RET_W), w0))
    u, q, k32, v32, _, _ = _inproj1(x2, gains[1, 0][None], w1["w_in1"], tm=tm, idt=F32)
    bb = _tile(nseq, (64, 32, 16, 8))
    o_s5, h_re, h_im = _s5(u.reshape(lnew, nseq, S5_CH), s5_re.reshape(nseq, S5_HID), s5_im.reshape(nseq, S5_HID),
                           ws5, tc=lnew, bb=bb, odt=F32)
    seq = lambda a: to_tm(a.reshape(lnew, nseq, SB_W))
    k_seq, v_seq = seq(k32), seq(v32)
    q_hd = seq(q).reshape(nseq, lnew, SB_HEADS, SB_DIM).transpose(0, 2, 1, 3)
    q_bd = jnp.einsum('bhtd,hg->bhtgd', q_hd, jnp.eye(SB_HEADS, dtype=F32)).reshape(nseq, SB_HEADS * lnew, SB_W)
    npage_sb = _tile(page_table.shape[1], (16, 8, 4, 2))
    o_sb = _sb_sample(page_table, q_bd, pad_keys(k_seq), pad_keys(v_seq), jnp.transpose(cache_k, (0, 2, 3, 1)),
                      jnp.transpose(cache_v, (0, 2, 3, 1)), npage=npage_sb)
    o_sb_tm = to_tm(o_sb).reshape(rows, SB_W)
    x4, cs1 = _layer_tail(x2, 1, gains, ffn, conv_tm(state_conv[1]), nb=nseq, tm=tmf,
                          mixer=(o_s5.reshape(rows, S5_CH), o_sb_tm, w1))
    y = to_tm(x4.reshape(lnew, nseq, D_MODEL))
    conv = jnp.stack([cs0, cs1]).reshape(2, CONV_W - 1, nseq, 2 * D_FF).transpose(0, 2, 1, 3)
    return (y, rows_seq, s_new.reshape(nseq, RET_HEADS, RET_DK, RET_DV), h_re.reshape(nseq, S5_GROUPS, S5_STATE),
            h_im.reshape(nseq, S5_GROUPS, S5_STATE), k_seq.reshape(nseq, lnew, SB_HEADS, SB_DIM),
            v_seq.reshape(nseq, lnew, SB_HEADS, SB_DIM), conv)


def kernel(x_prompt, x_sample, cache_mla, state_ret, state_s5_re, state_s5_im, cache_sb_k, cache_sb_v,
           state_ffn_conv, page_table, norm_gains, w_in_l0, mla_q_norm, mla_w_uq, mla_kv_norm, mla_w_uk,
           mla_w_uv, w_out_l0, w_in_l1, s5_lam_re, s5_lam_im, s5_log_dt, s5_b_re, s5_b_im, s5_c_re, s5_c_im,
           s5_d, s5_w_glu, w_out_l1, ffn_w_up, ffn_conv_w, ffn_conv_b, ffn_w_down):
    w0 = _prep_layer0(w_in_l0, mla_q_norm, mla_w_uq, mla_kv_norm, mla_w_uk, mla_w_uv, w_out_l0)
    w1 = dict(w_in1=w_in_l1.astype(BF16), w_out_a=w_out_l1[:S5_CH].astype(BF16),
              w_out_b=w_out_l1[S5_CH:].astype(BF16))
    ws5 = _prep_s5(s5_lam_re, s5_lam_im, s5_log_dt, s5_b_re, s5_b_im, s5_c_re, s5_c_im, s5_d, s5_w_glu)
    ffn = dict(w_up=ffn_w_up.astype(BF16), conv_w=ffn_conv_w, conv_b=ffn_conv_b, w_down=ffn_w_down.astype(BF16))
    gains = norm_gains.astype(F32)
    (y_p, mla_p, ret_p, s5r_p, s5i_p, sbk_p, sbv_p, conv_p) = _run_prompt(x_prompt, gains, w0, w1, ws5, ffn)
    (y_s, mla_s, ret_s, s5r_s, s5i_s, sbk_s, sbv_s, conv_s) = _run_sample(
        x_sample, gains, w0, w1, ws5, ffn, cache_mla, state_ret, state_s5_re, state_s5_im, cache_sb_k, cache_sb_v,
        state_ffn_conv, page_table)
    return (y_p, y_s, mla_p, mla_s, ret_p, ret_s, s5r_p, s5i_p, s5r_s, s5i_s, sbk_p, sbv_p, sbk_s, sbv_s,
            conv_p, conv_s)
```

```python
import functools
import math

import jax
import jax.numpy as jnp
import numpy as np
from jax import lax
from jax.experimental import pallas as pl
from jax.experimental.pallas import tpu as pltpu

F32 = jnp.float32
BF16 = jnp.bfloat16

D_MODEL = 1024
PAGE_SIZE = 128
MLA_HEADS = 8
MLA_Q_RANK = 384
MLA_KV_RANK = 256
MLA_NOPE = 64
MLA_ROPE = 32
MLA_V = 64
MLA_ROW = MLA_KV_RANK + MLA_ROPE
MLA_W = MLA_HEADS * MLA_V
RET_HEADS = 4
RET_DK = 64
RET_DV = 128
RET_W = RET_HEADS * RET_DV
RET_QK = RET_HEADS * RET_DK
RET_CHUNK = 128
S5_GROUP = 16
S5_CH = 640
S5_GROUPS = S5_CH // S5_GROUP
S5_STATE = 64
S5_HID = S5_GROUPS * S5_STATE
SB_HEADS = 6
SB_DIM = 64
SB_W = SB_HEADS * SB_DIM
D_FF = 2816
CONV_W = 3
ROPE_THETA = 10000.0
EPS = 1e-6
NEG = -1e30

VMEM_LIMIT_BYTES = 56 * 1024 * 1024
LANES = 128


def _cparams(sem):
    return pltpu.CompilerParams(dimension_semantics=sem, vmem_limit_bytes=VMEM_LIMIT_BYTES)


def _const_spec(shape):
    nd = len(shape)
    return pl.BlockSpec(shape, lambda *_: (0,) * nd, pipeline_mode=pl.Buffered(1))


def _rms(x):
    return x * lax.rsqrt(jnp.mean(x * x, axis=-1, keepdims=True) + EPS)


def _gelu_tanh(x):
    c = math.sqrt(2.0 / math.pi)
    return x * (0.5 * (1.0 + jnp.tanh(c * (x + 0.044715 * (x * x * x)))))


def _dot(a, b):
    return jnp.dot(a, b, preferred_element_type=F32)


def _dot_nt(a, b):
    return lax.dot_general(a, b, (((1,), (1,)), ((), ())), preferred_element_type=F32)


def _lane_band(x, lo, width):
    lane = lax.broadcasted_iota(jnp.int32, x.shape, x.ndim - 1)
    return jnp.where((lane >= lo) & (lane < lo + width), x, jnp.zeros_like(x))


_O_CQ, _O_CKV, _O_KPE, _O_KPES = 0, 384, 640, 896
_O_QR, _O_QRS, _O_KR, _O_KRS, _O_VR, _O_GR = 1152, 1408, 1664, 1920, 2176, 2688
IN0_AUG = 3200
LOG2E = math.log2(math.e)
_MLA_SCALE = (MLA_NOPE + MLA_ROPE) ** -0.5 * LOG2E


def _inproj0_kernel(x_ref, g_ref, win_ref, qn_ref, wuq_ref, kvn_ref, wuk_ref, cm_ref, sm_ref, cr_ref, sr_ref,
                    qlat_ref, qpe_ref, kfull_ref, rows_ref, qr_ref, kr_ref, vr_ref, sg_ref):
    xn = (_rms(x_ref[...]) * g_ref[...]).astype(BF16)
    z = _dot(xn, win_ref[...])
    cqn = (_rms(z[:, _O_CQ:_O_CQ + MLA_Q_RANK]) * qn_ref[...]).astype(BF16)
    q = _dot(cqn, wuq_ref[...])
    for p in range(MLA_HEADS // 2):
        ql = _dot(q[:, 128 * p:128 * (p + 1)].astype(BF16), wuk_ref[p]) * _MLA_SCALE
        qlat_ref[2 * p] = ql[:, :MLA_KV_RANK].astype(qlat_ref.dtype)
        qlat_ref[2 * p + 1] = ql[:, MLA_KV_RANK:].astype(qlat_ref.dtype)
    cm, sm = cm_ref[...], sm_ref[...]
    qpe = (q[:, 512:768] * cm + q[:, 768:1024] * sm) * _MLA_SCALE
    qpe_ref[...] = qpe.astype(qpe_ref.dtype)
    ckv = _rms(z[:, _O_CKV:_O_CKV + MLA_KV_RANK]) * kvn_ref[...]
    kpe = z[:, _O_KPE:_O_KPE + 256] * cm + z[:, _O_KPES:_O_KPES + 256] * sm
    kfull_ref[:, 0:256] = ckv.astype(kfull_ref.dtype)
    kfull_ref[:, 256:512] = kpe.astype(kfull_ref.dtype)
    rows_ref[:, 0:MLA_KV_RANK] = ckv
    rows_ref[:, MLA_KV_RANK:MLA_ROW] = kpe[:, 0:MLA_ROPE]
    cr, sr = cr_ref[...], sr_ref[...]
    qr_ref[...] = z[:, _O_QR:_O_QR + RET_QK] * cr + z[:, _O_QRS:_O_QRS + RET_QK] * sr
    kr_ref[...] = (z[:, _O_KR:_O_KR + RET_QK] * cr + z[:, _O_KRS:_O_KRS + RET_QK] * sr) * (RET_DK ** -0.5)
    vr_ref[...] = z[:, _O_VR:_O_VR + RET_W].astype(vr_ref.dtype)
    g = z[:, _O_GR:_O_GR + RET_W]
    sg_ref[...] = g * jax.nn.sigmoid(g)


def _inproj0(x2d, gain, w, tabs, *, nseq, seqlen, tm, idt):
    nt = seqlen // tm
    cm, sm, cr, sr = tabs
    row_in = lambda b, i: (b * nt + i, 0)
    tm_out = lambda b, i: (i, b)
    tab = lambda b, i: (i, 0)
    out_shape = (
        jax.ShapeDtypeStruct((MLA_HEADS, seqlen, nseq * 256), idt),
        jax.ShapeDtypeStruct((seqlen, nseq * 256), idt),
        jax.ShapeDtypeStruct((seqlen, nseq * 512), idt),
        jax.ShapeDtypeStruct((nseq * seqlen, MLA_ROW), F32),
        jax.ShapeDtypeStruct((seqlen, nseq * RET_QK), F32),
        jax.ShapeDtypeStruct((seqlen, nseq * RET_QK), F32),
        jax.ShapeDtypeStruct((seqlen, nseq * RET_W), idt),
        jax.ShapeDtypeStruct((seqlen, nseq * RET_W), F32),
    )
    out_specs = (
        pl.BlockSpec((MLA_HEADS, tm, 256), lambda b, i: (0, i, b)),
        pl.BlockSpec((tm, 256), tm_out),
        pl.BlockSpec((tm, 512), tm_out),
        pl.BlockSpec((tm, MLA_ROW), row_in),
        pl.BlockSpec((tm, RET_QK), tm_out),
        pl.BlockSpec((tm, RET_QK), tm_out),
        pl.BlockSpec((tm, RET_W), tm_out),
        pl.BlockSpec((tm, RET_W), tm_out),
    )
    in_specs = [
        pl.BlockSpec((tm, D_MODEL), row_in),
        _const_spec((1, D_MODEL)),
        _const_spec((D_MODEL, IN0_AUG)),
        _const_spec((1, MLA_Q_RANK)),
        _const_spec((MLA_Q_RANK, 1024)),
        _const_spec((1, MLA_KV_RANK)),
        _const_spec((MLA_HEADS // 2, 128, 512)),
        pl.BlockSpec((tm, 256), tab), pl.BlockSpec((tm, 256), tab),
        pl.BlockSpec((tm, 256), tab), pl.BlockSpec((tm, 256), tab),
    ]
    return pl.pallas_call(
        _inproj0_kernel, grid=(nseq, nt), in_specs=in_specs, out_specs=out_specs, out_shape=out_shape,
        compiler_params=_cparams(("parallel", "parallel")), name="inproj0",
    )(x2d, gain, w["w_in0"], w["q_norm"], w["w_uq"], w["kv_norm"], w["w_uk"], cm, sm, cr, sr)


def _mla_value_proj(o_lat, wuv_ref, n):
    pairs = []
    for p in range(MLA_HEADS // 2):
        pairs.append(_dot(o_lat[2 * p * n:(2 * p + 1) * n], wuv_ref[2 * p])
                     + _dot(o_lat[(2 * p + 1) * n:(2 * p + 2) * n], wuv_ref[2 * p + 1]))
    return jnp.concatenate(pairs, axis=1)


def _mla_prompt_kernel(qlat_ref, qpe_ref, k_ref, wuv_ref, o_ref, qs_ref, s_ref, mx_ref, ls_ref, acc_ref, *, tq):
    qi = pl.program_id(1)
    rows = MLA_HEADS * tq
    half = tq // 2
    assert half == LANES
    qpe = qpe_ref[...]
    for h in range(MLA_HEADS):
        qs_ref[h * tq:(h + 1) * tq, 0:256] = qlat_ref[h]
        qs_ref[h * tq:(h + 1) * tq, 256:512] = _lane_band(qpe, MLA_ROPE * h, MLA_ROPE)
    mx_ref[...] = jnp.full((rows, LANES), NEG, F32)
    ls_ref[...] = jnp.zeros((rows, LANES), F32)
    acc_ref[...] = jnp.zeros((rows, MLA_KV_RANK), F32)

    def kblock(j):
        return k_ref[pl.ds(pl.multiple_of(j * tq, tq), tq), :]

    def pass1(j, c):
        s = _dot_nt(qs_ref[...], kblock(j))
        s_ref[j] = s
        mx_ref[...] = jnp.maximum(mx_ref[...], jnp.maximum(s[:, :half], s[:, half:]))
        return c

    lax.fori_loop(0, qi, pass1, 0)
    t_q = lax.broadcasted_iota(jnp.int32, (rows, tq), 0) & (tq - 1)
    t_k = lax.broadcasted_iota(jnp.int32, (rows, tq), 1)
    s = jnp.where(t_k <= t_q, _dot_nt(qs_ref[...], kblock(qi)), NEG)
    s_ref[qi] = s
    mx = jnp.maximum(mx_ref[...], jnp.maximum(s[:, :half], s[:, half:]))
    mx_ref[...] = jnp.broadcast_to(jnp.max(mx, axis=-1, keepdims=True), (rows, LANES))

    def pass2(j, c):
        s = s_ref[j]
        m = mx_ref[...]
        p0 = jnp.exp2(s[:, :half] - m)
        p1 = jnp.exp2(s[:, half:] - m)
        ls_ref[...] += p0 + p1
        p = jnp.concatenate([p0, p1], axis=1).astype(BF16)
        acc_ref[...] += _dot(p, kblock(j)[:, 0:MLA_KV_RANK])
        return c

    lax.fori_loop(0, qi + 1, pass2, 0)
    l = jnp.sum(ls_ref[...], axis=-1, keepdims=True)
    o_lat = (acc_ref[...] / l).astype(BF16)
    o_ref[...] = _mla_value_proj(o_lat, wuv_ref, tq).astype(o_ref.dtype)


def _mla_prompt(qlat, qpe, kfull, wuv, *, nseq, seqlen, tq):
    nq = seqlen // tq
    rows = MLA_HEADS * tq
    assert tq & (tq - 1) == 0
    return pl.pallas_call(
        functools.partial(_mla_prompt_kernel, tq=tq),
        grid=(nseq, nq),
        in_specs=[
            pl.BlockSpec((MLA_HEADS, tq, 256), lambda b, i: (0, i, b)),
            pl.BlockSpec((tq, 256), lambda b, i: (i, b)),
            pl.BlockSpec((seqlen, 512), lambda b, i: (0, b)),
            _const_spec((MLA_HEADS, MLA_KV_RANK, 2 * MLA_V)),
        ],
        out_specs=pl.BlockSpec((tq, MLA_W), lambda b, i: (i, b)),
        out_shape=jax.ShapeDtypeStruct((seqlen, nseq * MLA_W), BF16),
        scratch_shapes=[pltpu.VMEM((rows, 512), BF16), pltpu.VMEM((nq, rows, tq), F32),
                        pltpu.VMEM((rows, LANES), F32), pltpu.VMEM((rows, LANES), F32),
                        pltpu.VMEM((rows, MLA_KV_RANK), F32)],
        compiler_params=_cparams(("parallel", "parallel")), name="mla_prompt",
    )(qlat, qpe, kfull, wuv)


def _mla_sample_kernel(pt_ref, qs_ref, knt_ref, wuv_ref, *refs, npage, nstep, lnew):
    pages = refs[:npage]
    o_ref = refs[npage]
    m_ref, l_ref, acc_ref = refs[npage + 1:]
    step = pl.program_id(1)
    rows = MLA_HEADS * lnew

    @pl.when(step == 0)
    def _():
        m_ref[...] = jnp.full((rows, 1), NEG, F32)
        l_ref[...] = jnp.zeros((rows, 1), F32)
        acc_ref[...] = jnp.zeros((rows, MLA_KV_RANK), F32)

    q = qs_ref[0].astype(BF16)

    def update(s, kts):
        m_prev = m_ref[...]
        m_new = jnp.maximum(m_prev, jnp.max(s, axis=-1, keepdims=True))
        alpha = jnp.exp2(m_prev - m_new)
        p = jnp.exp2(s - m_new)
        l_ref[...] = alpha * l_ref[...] + jnp.sum(p, axis=-1, keepdims=True)
        pv = _dot_nt(p[:, 0:PAGE_SIZE].astype(BF16), kts[0][0:MLA_KV_RANK])
        for n in range(1, len(kts)):
            pv = pv + _dot_nt(p[:, n * PAGE_SIZE:(n + 1) * PAGE_SIZE].astype(BF16), kts[n][0:MLA_KV_RANK])
        acc_ref[...] = alpha * acc_ref[...] + pv
        m_ref[...] = m_new

    kts = [pages[n][0].astype(BF16) for n in range(npage)]
    update(jnp.concatenate([_dot(q, kt) for kt in kts], axis=1), kts)

    @pl.when(step == nstep - 1)
    def _():
        knt = knt_ref[0].astype(BF16)
        sn = _dot(q, knt)
        t_q = lax.broadcasted_iota(jnp.int32, (rows, PAGE_SIZE), 0) & (lnew - 1)
        t_k = lax.broadcasted_iota(jnp.int32, (rows, PAGE_SIZE), 1)
        update(jnp.where(t_k <= t_q, sn, NEG), [knt])
        o_lat = (acc_ref[...] / l_ref[...]).astype(BF16)
        o_ref[0] = _mla_value_proj(o_lat, wuv_ref, lnew)


def _mla_sample(page_table, qs, knew_t, wuv, cache_t, *, npage):
    nseq, n_pages = page_table.shape
    rows = qs.shape[1]
    lnew = rows // MLA_HEADS
    assert lnew & (lnew - 1) == 0 and lnew <= PAGE_SIZE
    nstep = n_pages // npage
    page_specs = [
        pl.BlockSpec((1, MLA_ROW, PAGE_SIZE), lambda b, s, pt, n=n: (pt[b, s * npage + n], 0, 0))
        for n in range(npage)
    ]
    grid_spec = pltpu.PrefetchScalarGridSpec(
        num_scalar_prefetch=1, grid=(nseq, nstep),
        in_specs=[
            pl.BlockSpec((1, rows, MLA_ROW), lambda b, s, pt: (b, 0, 0)),
            pl.BlockSpec((1, MLA_ROW, PAGE_SIZE), lambda b, s, pt: (b, 0, 0)),
            pl.BlockSpec((MLA_HEADS, MLA_KV_RANK, 2 * MLA_V), lambda b, s, pt: (0, 0, 0)),
        ] + page_specs,
        out_specs=pl.BlockSpec((1, lnew, MLA_W), lambda b, s, pt: (b, 0, 0)),
        scratch_shapes=[pltpu.VMEM((rows, 1), F32), pltpu.VMEM((rows, 1), F32),
                        pltpu.VMEM((rows, MLA_KV_RANK), F32)],
    )
    return pl.pallas_call(
        functools.partial(_mla_sample_kernel, npage=npage, nstep=nstep, lnew=lnew),
        grid_spec=grid_spec,
        out_shape=jax.ShapeDtypeStruct((nseq, lnew, MLA_W), F32),
        compiler_params=_cparams(("parallel", "arbitrary")), name="mla_sample",
    )(page_table, qs, knew_t, wuv, *([cache_t] * npage))


def _retention_kernel(q_ref, k_ref, v_ref, sg_ref, intra_ref, qdec_ref, kdec_ref, cdec_ref, s0_ref,
                      o_ref, so_ref, s_ref, pad_ref, *, c, nchunk, lrows):
    i = pl.program_id(1)

    @pl.when(i == 0)
    def _():
        s_ref[...] = s0_ref[0]

    row = lax.broadcasted_iota(jnp.int32, (RET_QK, RET_DV), 0)
    for ch in range(nchunk):
        if lrows < c:
            pad_ref[...] = jnp.zeros(pad_ref.shape, F32)
            pad_ref[0:lrows, 0:RET_QK] = q_ref[...]
            pad_ref[0:lrows, RET_QK:2 * RET_QK] = k_ref[...]
            pad_ref[0:lrows, 2 * RET_QK:2 * RET_QK + RET_W] = v_ref[...].astype(F32)
            q = pad_ref[:, 0:RET_QK]
            k = pad_ref[:, RET_QK:2 * RET_QK]
            v_all = pad_ref[:, 2 * RET_QK:2 * RET_QK + RET_W].astype(BF16)
        else:
            rs = slice(ch * c, (ch + 1) * c)
            q, k, v_all = q_ref[rs, :], k_ref[rs, :], v_ref[rs, :].astype(BF16)
        qd = q * qdec_ref[...]
        kdt = (k * kdec_ref[...]).T.astype(BF16)
        kb = k.astype(BF16)
        s_old = s_ref[...]
        sb = s_old.astype(BF16)
        s_new = cdec_ref[...] * s_old
        for h in range(RET_HEADS):
            qh = _lane_band(q, RET_DK * h, RET_DK).astype(BF16)
            qdh = _lane_band(qd, RET_DK * h, RET_DK).astype(BF16)
            vh = v_all[:, RET_DV * h:RET_DV * (h + 1)]
            att = _dot_nt(qh, kb) * intra_ref[h]
            o = _dot(att.astype(BF16), vh) + _dot(qdh, sb)
            o = _rms(o)
            if lrows < c:
                o_ref[:, RET_DV * h:RET_DV * (h + 1)] = (
                    o[0:lrows] * sg_ref[:, RET_DV * h:RET_DV * (h + 1)]).astype(o_ref.dtype)
            else:
                o_ref[rs, RET_DV * h:RET_DV * (h + 1)] = (
                    o * sg_ref[rs, RET_DV * h:RET_DV * (h + 1)]).astype(o_ref.dtype)
            upd = _dot(kdt, vh)
            s_new = s_new + jnp.where((row >= RET_DK * h) & (row < RET_DK * (h + 1)), upd, 0.0)
        s_ref[...] = s_new

    @pl.when(i == pl.num_programs(1) - 1)
    def _():
        so_ref[0] = s_ref[...]


def _retention_tables(c_eff, c_pad):
    log_g = jnp.log(1.0 - 2.0 ** (-5.0 - jnp.arange(RET_HEADS, dtype=F32)))
    i = jnp.arange(c_pad, dtype=F32)
    diff = i[:, None] - i[None, :]
    intra = jnp.where(diff >= 0, jnp.exp(jnp.maximum(diff, 0.0)[None] * log_g[:, None, None]), 0.0)
    q_dec = jnp.exp((i[:, None] + 1.0) * log_g[None, :])
    k_dec = jnp.exp(jnp.maximum(c_eff - 1.0 - i, 0.0)[:, None] * log_g[None, :])
    c_dec = jnp.exp(c_eff * log_g)
    rep = lambda a: jnp.repeat(a, RET_DK, axis=1)
    cdec_rows = jnp.broadcast_to(jnp.repeat(c_dec, RET_DK)[:, None], (RET_QK, RET_DV))
    return intra.astype(F32), rep(q_dec), rep(k_dec), cdec_rows


def _retention(qr, kr, vr, sg, s0, *, nseq, seqlen, odt):
    c = RET_CHUNK
    if seqlen % c == 0:
        c_eff, tr = c, _tile(seqlen, (8 * c, 4 * c, 2 * c, c))
        nchunk, lrows = tr // c, tr
    else:
        c_eff, tr, nchunk, lrows = seqlen, seqlen, 1, seqlen
    nt = seqlen // tr
    intra, qdec, kdec, cdec = _retention_tables(c_eff, c)
    blk = lambda w: pl.BlockSpec((tr, w), lambda b, i: (i, b))
    return pl.pallas_call(
        functools.partial(_retention_kernel, c=c, nchunk=nchunk, lrows=lrows),
        grid=(nseq, nt),
        in_specs=[blk(RET_QK), blk(RET_QK), blk(RET_W), blk(RET_W),
                  _const_spec((RET_HEADS, c, c)), _const_spec((c, RET_QK)), _const_spec((c, RET_QK)),
                  _const_spec((RET_QK, RET_DV)),
                  pl.BlockSpec((1, RET_QK, RET_DV), lambda b, i: (b, 0, 0))],
        out_specs=(blk(RET_W), pl.BlockSpec((1, RET_QK, RET_DV), lambda b, i: (b, 0, 0))),
        out_shape=(jax.ShapeDtypeStruct((seqlen, nseq * RET_W), odt),
                   jax.ShapeDtypeStruct((nseq, RET_QK, RET_DV), F32)),
        scratch_shapes=[pltpu.VMEM((RET_QK, RET_DV), F32), pltpu.VMEM((c, 2 * RET_QK + RET_W), F32)],
        compiler_params=_cparams(("parallel", "arbitrary")), name="retention",
    )(qr, kr, vr, sg, intra, qdec, kdec, cdec, s0)


def _outproj_kernel(x_ref, a_ref, b_ref, wa_ref, wb_ref, g_ref, y_ref):
    m = _dot(a_ref[...].astype(BF16), wa_ref[...]) + _dot(b_ref[...].astype(BF16), wb_ref[...])
    y_ref[...] = x_ref[...] + _rms(m) * g_ref[...]


def _outproj(x, a, b, wa, wb, gain, *, grid, x_map, ab_map, y_map, y_shape, tm):
    wa_w, wb_w = wa.shape[0], wb.shape[0]
    return pl.pallas_call(
        _outproj_kernel, grid=grid,
        in_specs=[pl.BlockSpec((tm, D_MODEL), x_map), pl.BlockSpec((tm, wa_w), ab_map),
                  pl.BlockSpec((tm, wb_w), ab_map), _const_spec(wa.shape), _const_spec(wb.shape),
                  _const_spec((1, D_MODEL))],
        out_specs=pl.BlockSpec((tm, D_MODEL), y_map),
        out_shape=jax.ShapeDtypeStruct(y_shape, F32),
        compiler_params=_cparams(("parallel",) * len(grid)), name="outproj",
    )(x, a, b, wa, wb, gain)


_FFN_CK = 256


def _ffn_kernel(*refs, nb, tm, mixer):
    if mixer:
        x_ref, a_ref, b_ref, wa_ref, wb_ref, g1_ref = refs[:6]
        refs = refs[6:]
    else:
        x_ref = refs[0]
        refs = refs[1:]
    g2_ref, wup_ref, cw_ref, cb_ref, wdn_ref, g3_ref, st_ref, y_ref, cs_ref, carry_ref = refs
    i = pl.program_id(0)
    ck = _FFN_CK

    @pl.when(i == 0)
    def _():
        carry_ref[...] = st_ref[...]

    x = x_ref[...]
    if mixer:
        m = _dot(a_ref[...].astype(BF16), wa_ref[...]) + _dot(b_ref[...].astype(BF16), wb_ref[...])
        x = x + _rms(m) * g1_ref[...]
    xn = (_rms(x) * g2_ref[...]).astype(BF16)
    acc = jnp.zeros((tm, D_MODEL), F32)
    for c in range(D_FF // ck):
        act = []
        for half in range(2):
            col = half * D_FF + c * ck
            cs = slice(col, col + ck)
            h = _dot(xn, wup_ref[:, cs])
            prev2, prev1 = carry_ref[0:nb, cs], carry_ref[nb:2 * nb, cs]
            carry_ref[:, cs] = h[tm - 2 * nb:tm]
            hm2 = jnp.concatenate([prev2, prev1, h[0:tm - 2 * nb]], axis=0)
            hm1 = jnp.concatenate([prev1, h[0:tm - nb]], axis=0)
            act.append(cb_ref[:, cs] + cw_ref[0:1, cs] * hm2 + cw_ref[1:2, cs] * hm1 + cw_ref[2:3, cs] * h)
        a = (_gelu_tanh(act[0]) * act[1]).astype(BF16)
        acc = acc + _dot(a, wdn_ref[c * ck:(c + 1) * ck, :])
    y_ref[...] = x + _rms(acc) * g3_ref[...]

    @pl.when(i == pl.num_programs(0) - 1)
    def _():
        cs_ref[...] = carry_ref[...]


def _ffn(x, g2, wup, cw, cb, wdn, g3, state, *, nb, tm, mixer=None):
    rows = x.shape[0]
    assert tm >= 2 * nb and tm % nb == 0 and rows % tm == 0
    blk = lambda w: pl.BlockSpec((tm, w), lambda i: (i, 0))
    ins, specs = [x], [blk(D_MODEL)]
    if mixer is not None:
        a, b, wa, wb, g1 = mixer
        ins += [a, b, wa, wb, g1]
        specs += [blk(a.shape[1]), blk(b.shape[1]), _const_spec(wa.shape), _const_spec(wb.shape),
                  _const_spec((1, D_MODEL))]
    ins += [g2, wup, cw, cb, wdn, g3, state]
    specs += [_const_spec((1, D_MODEL)), _const_spec((D_MODEL, 2 * D_FF)), _const_spec((CONV_W, 2 * D_FF)),
              _const_spec((1, 2 * D_FF)), _const_spec((D_FF, D_MODEL)), _const_spec((1, D_MODEL)),
              _const_spec((2 * nb, 2 * D_FF))]
    return pl.pallas_call(
        functools.partial(_ffn_kernel, nb=nb, tm=tm, mixer=mixer is not None),
        grid=(rows // tm,),
        in_specs=specs,
        out_specs=(blk(D_MODEL), pl.BlockSpec((2 * nb, 2 * D_FF), lambda i: (0, 0))),
        out_shape=(jax.ShapeDtypeStruct((rows, D_MODEL), F32), jax.ShapeDtypeStruct((2 * nb, 2 * D_FF), F32)),
        scratch_shapes=[pltpu.VMEM((2 * nb, 2 * D_FF), F32)],
        compiler_params=_cparams(("arbitrary",)), name="convffn",
    )(*ins)


def _inproj1_kernel(x_ref, g_ref, win_ref, u_ref, q_ref, k_ref, v_ref, kb_ref, vb_ref):
    xn = (_rms(x_ref[...]) * g_ref[...]).astype(BF16)
    z = _dot(xn, win_ref[...])
    u_ref[...] = z[:, 0:S5_CH]
    q_ref[...] = (z[:, S5_CH:S5_CH + SB_W] * (SB_DIM ** -0.5)).astype(q_ref.dtype)
    k = z[:, S5_CH + SB_W:S5_CH + 2 * SB_W]
    v = z[:, S5_CH + 2 * SB_W:S5_CH + 3 * SB_W]
    k_ref[...] = k
    v_ref[...] = v
    kb_ref[...] = k.astype(kb_ref.dtype)
    vb_ref[...] = v.astype(vb_ref.dtype)


def _inproj1(x, gain, win, *, tm, idt):
    rows = x.shape[0]
    blk = lambda w: pl.BlockSpec((tm, w), lambda i: (i, 0))
    sds = lambda w, dt: jax.ShapeDtypeStruct((rows, w), dt)
    return pl.pallas_call(
        _inproj1_kernel, grid=(rows // tm,),
        in_specs=[blk(D_MODEL), _const_spec((1, D_MODEL)), _const_spec(win.shape)],
        out_specs=(blk(S5_CH), blk(SB_W), blk(SB_W), blk(SB_W), blk(SB_W), blk(SB_W)),
        out_shape=(sds(S5_CH, F32), sds(SB_W, idt), sds(SB_W, F32), sds(SB_W, F32), sds(SB_W, idt), sds(SB_W, idt)),
        compiler_params=_cparams(("parallel",)), name="inproj1",
    )(x, gain, win)


_S5_LW = 256
_S5_CB = 128
_S5_NBLK = S5_CH // _S5_CB
_S5_HB = S5_HID // _S5_NBLK


def _s5_kernel(u_ref, h0r_ref, h0i_ref, br_ref, bi_ref, ar_ref, ai_ref, cr_ref, cin_ref, d_ref, wg_ref,
               o_ref, hr_ref, hi_ref, *, tc, bb):
    i = pl.program_id(1)

    @pl.when(i == 0)
    def _():
        hr_ref[...] = h0r_ref[...]
        hi_ref[...] = h0i_ref[...]

    u = u_ref[...].reshape(tc * bb, S5_CH)
    ub = u.astype(BF16)
    ys = []
    for m in range(_S5_NBLK):
        um = ub[:, _S5_CB * m:_S5_CB * (m + 1)]
        sr, si = _dot(um, br_ref[m]), _dot(um, bi_ref[m])
        hre, him = [], []
        for sub in range(_S5_HB // _S5_LW):
            ls = slice(sub * _S5_LW, (sub + 1) * _S5_LW)
            cs = slice(_S5_HB * m + sub * _S5_LW, _S5_HB * m + (sub + 1) * _S5_LW)
            ar, ai = ar_ref[:, cs], ai_ref[:, cs]
            hr, hi = hr_ref[:, cs], hi_ref[:, cs]
            hrs, his = [], []
            for t in range(tc):
                rs = slice(t * bb, (t + 1) * bb)
                hr, hi = ar * hr - ai * hi + sr[rs, ls], ar * hi + ai * hr + si[rs, ls]
                hrs.append(hr)
                his.append(hi)
            hr_ref[:, cs] = hr
            hi_ref[:, cs] = hi
            hre.append(jnp.concatenate(hrs, axis=0))
            him.append(jnp.concatenate(his, axis=0))
        ys.append(_dot(jnp.concatenate(hre, axis=1).astype(BF16), cr_ref[m])
                  + _dot(jnp.concatenate(him, axis=1).astype(BF16), cin_ref[m]))
    y = jnp.concatenate(ys, axis=1)
    y = _gelu_tanh(y + d_ref[...] * u)
    o = y * jax.nn.sigmoid(_dot(y.astype(BF16), wg_ref[...]))
    o_ref[...] = o.reshape(tc, bb, S5_CH).astype(o_ref.dtype)


def _s5(u3, h0r, h0i, w, *, tc, bb, odt):
    tlen, nseq, _ = u3.shape
    ublk = pl.BlockSpec((tc, bb, S5_CH), lambda j, i: (i, j, 0))
    hblk = pl.BlockSpec((bb, S5_HID), lambda j, i: (j, 0))
    return pl.pallas_call(
        functools.partial(_s5_kernel, tc=tc, bb=bb),
        grid=(nseq // bb, tlen // tc),
        in_specs=[ublk, hblk, hblk, _const_spec((_S5_NBLK, _S5_CB, _S5_HB)), _const_spec((_S5_NBLK, _S5_CB, _S5_HB)),
                  _const_spec((1, S5_HID)), _const_spec((1, S5_HID)), _const_spec((_S5_NBLK, _S5_HB, _S5_CB)),
                  _const_spec((_S5_NBLK, _S5_HB, _S5_CB)), _const_spec((1, S5_CH)), _const_spec((S5_CH, S5_CH))],
        out_specs=(ublk, hblk, hblk),
        out_shape=(jax.ShapeDtypeStruct((tlen, nseq, S5_CH), odt), jax.ShapeDtypeStruct((nseq, S5_HID), F32),
                   jax.ShapeDtypeStruct((nseq, S5_HID), F32)),
        compiler_params=_cparams(("parallel", "arbitrary")), name="s5",
    )(u3, h0r, h0i, w["bbr"], w["bbi"], w["abr"], w["abi"], w["ccr"], w["ccin"], w["d"], w["w_glu"])


def _sb_weights(z, tri, carry, valid):
    lsz, lk, cum = _sb_logs(z, tri, valid)
    w = jnp.exp(lsz + (cum + carry))
    if valid is not None:
        w = jnp.where(valid, w, 0.0)
    return w.astype(BF16), jnp.sum(lk, axis=-1, keepdims=True)


def _sb_logs(z, tri, valid):
    t = jnp.log(1.0 + jnp.exp2(jnp.abs(z) * (-LOG2E)))
    lsz = jnp.minimum(z, 0.0) - t
    lk = lsz - z
    if valid is not None:
        lk = jnp.where(valid, lk, 0.0)
    hi = lk.astype(BF16)
    lo = (lk - hi.astype(F32)).astype(BF16)
    return lsz, lk, _dot(jnp.concatenate([hi, lo], axis=1), tri)


def _sb_prompt_kernel(q_ref, kb_ref, vb_ref, k32_ref, v32_ref, tri_ref, o_ref, ko_ref, vo_ref,
                      qm_ref, carry_ref, acc_ref, *, tq):
    qi = pl.program_id(1)
    rows = SB_HEADS * tq
    ko_ref[...] = k32_ref[...]
    vo_ref[...] = v32_ref[...]
    tri = tri_ref[...]
    for h in range(SB_HEADS):
        qm_ref[h] = _lane_band(q_ref[:, LANES * (h // 2):LANES * (h // 2 + 1)], SB_DIM * (h % 2), SB_DIM)
    carry_ref[...] = jnp.zeros(carry_ref.shape, F32)
    acc_ref[...] = jnp.zeros(acc_ref.shape, F32)

    def visit(j, valid):
        rs = pl.ds(pl.multiple_of(j * tq, tq), tq)
        ps = [slice(LANES * (h // 2), LANES * (h // 2 + 1)) for h in range(SB_HEADS)]
        z = jnp.concatenate([_dot_nt(qm_ref[h], kb_ref[rs, ps[h]]) for h in range(SB_HEADS)], axis=0)
        w, lksum = _sb_weights(z, tri, carry_ref[...], valid)
        for h in range(SB_HEADS):
            acc_ref[h] += _dot(w[h * tq:(h + 1) * tq], vb_ref[rs, ps[h]])
        carry_ref[...] += lksum

    t_q = lax.broadcasted_iota(jnp.int32, (rows, tq), 0) & (tq - 1)
    t_k = lax.broadcasted_iota(jnp.int32, (rows, tq), 1)
    visit(qi, t_k < t_q)

    def body(jj, c):
        visit(qi - 1 - jj, None)
        return c

    lax.fori_loop(0, qi, body, 0)
    lane = lax.broadcasted_iota(jnp.int32, (tq, LANES), 1)
    for p in range(SB_HEADS // 2):
        o_ref[:, LANES * p:LANES * (p + 1)] = jnp.where(
            lane < SB_DIM, acc_ref[2 * p], acc_ref[2 * p + 1]).astype(o_ref.dtype)


def _tri(n):
    j = jnp.arange(n)
    t = (j[:, None] > j[None, :]).astype(BF16)
    return jnp.concatenate([t, t], axis=0)


def _sb_prompt(q, kb, vb, k32, v32, *, nseq, seqlen, tq):
    nq = seqlen // tq
    colblk = pl.BlockSpec((tq, SB_W), lambda b, i: (i, b))
    seqblk = pl.BlockSpec((seqlen, SB_W), lambda b, i: (0, b))
    rowblk = pl.BlockSpec((tq, SB_W), lambda b, i: (b * nq + i, 0))
    return pl.pallas_call(
        functools.partial(_sb_prompt_kernel, tq=tq),
        grid=(nseq, nq),
        in_specs=[colblk, seqblk, seqblk, colblk, colblk, _const_spec((2 * tq, tq))],
        out_specs=(colblk, rowblk, rowblk),
        out_shape=(jax.ShapeDtypeStruct((seqlen, nseq * SB_W), BF16),
                   jax.ShapeDtypeStruct((nseq * seqlen, SB_W), F32),
                   jax.ShapeDtypeStruct((nseq * seqlen, SB_W), F32)),
        scratch_shapes=[pltpu.VMEM((SB_HEADS, tq, LANES), BF16), pltpu.VMEM((SB_HEADS * tq, 1), F32),
                        pltpu.VMEM((SB_HEADS, tq, LANES), F32)],
        compiler_params=_cparams(("parallel", "parallel")), name="sb_prompt",
    )(q, kb, vb, k32, v32, _tri(tq))


def _sb_sample_kernel(pt_ref, q_ref, knt_ref, vnt_ref, tri_ref, *refs, npage, nstep, lnew):
    kpages = refs[:npage]
    vpages = refs[npage:2 * npage]
    o_ref = refs[2 * npage]
    carry_ref, acc_ref = refs[2 * npage + 1:]
    step = pl.program_id(1)
    rows = SB_HEADS * lnew
    tri = tri_ref[...]
    q = q_ref[0].astype(BF16)

    @pl.when(step == 0)
    def _():
        t_q = lax.broadcasted_iota(jnp.int32, (rows, PAGE_SIZE), 0) & (lnew - 1)
        t_k = lax.broadcasted_iota(jnp.int32, (rows, PAGE_SIZE), 1)
        w, lksum = _sb_weights(_dot(q, knt_ref[0].astype(BF16)), tri, jnp.zeros((rows, 1), F32), t_k < t_q)
        acc_ref[...] = _dot_nt(w, vnt_ref[0].astype(BF16))
        carry_ref[...] = lksum

    z = jnp.concatenate([_dot(q, kpages[n][0].reshape(SB_W, PAGE_SIZE).astype(BF16)) for n in range(npage)],
                        axis=0)
    lsz, lk, cum = _sb_logs(z, tri, None)
    lksum = jnp.sum(lk, axis=-1, keepdims=True)
    carries, carry = [], carry_ref[...]
    for n in range(npage):
        carries.append(carry)
        carry = carry + lksum[n * rows:(n + 1) * rows]
    carry_ref[...] = carry
    w = jnp.exp(lsz + cum + jnp.concatenate(carries, axis=0)).astype(BF16)
    pv = _dot_nt(w[0:rows], vpages[0][0].reshape(SB_W, PAGE_SIZE).astype(BF16))
    for n in range(1, npage):
        pv = pv + _dot_nt(w[n * rows:(n + 1) * rows], vpages[n][0].reshape(SB_W, PAGE_SIZE).astype(BF16))
    acc_ref[...] += pv

    @pl.when(step == nstep - 1)
    def _():
        acc = acc_ref[...]
        out = _lane_band(acc[0:lnew], 0, SB_DIM)
        for h in range(1, SB_HEADS):
            out = out + _lane_band(acc[h * lnew:(h + 1) * lnew], SB_DIM * h, SB_DIM)
        o_ref[0] = out


def _sb_sample(page_table, q_bd, knew_t, vnew_t, cache_kt, cache_vt, *, npage):
    nseq, n_pages = page_table.shape
    rows = q_bd.shape[1]
    lnew = rows // SB_HEADS
    assert lnew & (lnew - 1) == 0 and lnew <= PAGE_SIZE
    nstep = n_pages // npage
    pidx = lambda b, s, pt, n: (pt[b, n_pages - 1 - (s * npage + n)], 0, 0, 0)
    pspecs = [pl.BlockSpec((1, SB_HEADS, SB_DIM, PAGE_SIZE), functools.partial(pidx, n=n)) for n in range(npage)]
    grid_spec = pltpu.PrefetchScalarGridSpec(
        num_scalar_prefetch=1, grid=(nseq, nstep),
        in_specs=[
            pl.BlockSpec((1, rows, SB_W), lambda b, s, pt: (b, 0, 0)),
            pl.BlockSpec((1, SB_W, PAGE_SIZE), lambda b, s, pt: (b, 0, 0)),
            pl.BlockSpec((1, SB_W, PAGE_SIZE), lambda b, s, pt: (b, 0, 0)),
            pl.BlockSpec((2 * PAGE_SIZE, PAGE_SIZE), lambda b, s, pt: (0, 0)),
        ] + pspecs + pspecs,
        out_specs=pl.BlockSpec((1, lnew, SB_W), lambda b, s, pt: (b, 0, 0)),
        scratch_shapes=[pltpu.VMEM((rows, 1), F32), pltpu.VMEM((rows, SB_W), F32)],
    )
    return pl.pallas_call(
        functools.partial(_sb_sample_kernel, npage=npage, nstep=nstep, lnew=lnew),
        grid_spec=grid_spec,
        out_shape=jax.ShapeDtypeStruct((nseq, lnew, SB_W), F32),
        compiler_params=_cparams(("parallel", "arbitrary")), name="sb_sample",
    )(page_table, q_bd, knew_t, vnew_t, _tri(PAGE_SIZE), *([cache_kt] * npage), *([cache_vt] * npage))


def _copy_kernel(x_ref, y_ref):
    y_ref[...] = x_ref[...]


def _to_seq_major(x_tm, *, nseq, seqlen, tm):
    nt = seqlen // tm
    return pl.pallas_call(
        _copy_kernel, grid=(nseq, nt),
        in_specs=[pl.BlockSpec((tm, D_MODEL), lambda b, i: (i, b))],
        out_specs=pl.BlockSpec((tm, D_MODEL), lambda b, i: (b * nt + i, 0)),
        out_shape=jax.ShapeDtypeStruct((nseq * seqlen, D_MODEL), F32),
        compiler_params=_cparams(("parallel", "parallel")), name="to_seq_major",
    )(x_tm)


def _swap_halves(n_heads, dim):
    idx = np.arange(n_heads * dim)
    return (idx // dim) * dim + (idx % dim + dim // 2) % dim


def _rope_tables(pos, half, reps):
    freqs = ROPE_THETA ** (-jnp.arange(half, dtype=F32) / half)
    ang = pos.astype(F32)[:, None] * freqs[None, :]
    c, s = jnp.cos(ang), jnp.sin(ang)
    return (jnp.tile(jnp.concatenate([c, c], axis=-1), (1, reps)),
            jnp.tile(jnp.concatenate([-s, s], axis=-1), (1, reps)))


def _prep_layer0(w_in, q_norm, w_uq, kv_norm, w_uk, w_uv, w_out):
    o_kpe = MLA_Q_RANK + MLA_KV_RANK
    o_qr = o_kpe + MLA_ROPE
    o_kr = o_qr + RET_QK
    o_v = o_kr + RET_QK
    cols = np.concatenate([
        np.arange(0, o_kpe),
        np.tile(o_kpe + np.arange(MLA_ROPE), MLA_HEADS), np.tile(o_kpe + _swap_halves(1, MLA_ROPE), MLA_HEADS),
        o_qr + np.arange(RET_QK), o_qr + _swap_halves(RET_HEADS, RET_DK),
        o_kr + np.arange(RET_QK), o_kr + _swap_halves(RET_HEADS, RET_DK),
        np.arange(o_v, o_v + 2 * RET_W)])
    assert cols.shape[0] == IN0_AUG
    hd = MLA_NOPE + MLA_ROPE
    heads = np.arange(MLA_HEADS)[:, None] * hd
    uq_cols = np.concatenate([
        (heads + np.arange(MLA_NOPE)[None, :]).ravel(),
        (heads + MLA_NOPE + np.arange(MLA_ROPE)[None, :]).ravel(),
        (heads + MLA_NOPE + _swap_halves(1, MLA_ROPE)[None, :]).ravel()])
    wuk_t = jnp.transpose(w_uk, (1, 2, 0))
    z = jnp.zeros((MLA_NOPE, MLA_KV_RANK), F32)
    wuk_pair = jnp.stack([jnp.block([[wuk_t[2 * p], z], [z, wuk_t[2 * p + 1]]]) for p in range(MLA_HEADS // 2)])
    wuv_pad = jnp.zeros((MLA_HEADS, MLA_KV_RANK, 2 * MLA_V), F32)
    for h in range(MLA_HEADS):
        wuv_pad = wuv_pad.at[h, :, MLA_V * (h % 2):MLA_V * (h % 2 + 1)].set(w_uv[:, h, :])
    return dict(w_in0=w_in[:, cols].astype(BF16), q_norm=q_norm[None, :], w_uq=w_uq[:, uq_cols].astype(BF16),
                kv_norm=kv_norm[None, :], w_uk=wuk_pair.astype(BF16), w_uv=wuv_pad.astype(BF16),
                w_out_a=w_out[:MLA_W].astype(BF16), w_out_b=w_out[MLA_W:].astype(BF16))


def _prep_s5(lam_re, lam_im, log_dt, b_re, b_im, c_re, c_im, d_skip, w_glu):
    lr, li = lam_re.astype(F32), lam_im.astype(F32)
    dt = jnp.exp(log_dt.astype(F32))[:, None]
    mag = jnp.exp(lr * dt)
    ab_re, ab_im = mag * jnp.cos(li * dt), mag * jnp.sin(li * dt)
    nr, ni = ab_re - 1.0, ab_im
    den = lr * lr + li * li
    f_re = (nr * lr + ni * li) / den
    f_im = (ni * lr - nr * li) / den
    bb_re = f_re[..., None] * b_re - f_im[..., None] * b_im
    bb_im = f_re[..., None] * b_im + f_im[..., None] * b_re
    gpb = _S5_CB // S5_GROUP
    eye = jnp.eye(gpb, dtype=F32)
    bd_in = lambda bb: jnp.einsum('mgpc,gh->mgchp', bb.reshape(_S5_NBLK, gpb, S5_STATE, S5_GROUP),
                                  eye).reshape(_S5_NBLK, _S5_CB, _S5_HB)
    bd_out = lambda cc: jnp.einsum('mgcp,gh->mgphc', cc.reshape(_S5_NBLK, gpb, S5_GROUP, S5_STATE),
                                   eye).reshape(_S5_NBLK, _S5_HB, _S5_CB)
    return dict(bbr=bd_in(bb_re).astype(BF16), bbi=bd_in(bb_im).astype(BF16),
                abr=ab_re.reshape(1, S5_HID), abi=ab_im.reshape(1, S5_HID),
                ccr=bd_out(c_re).astype(BF16), ccin=bd_out(-c_im).astype(BF16),
                d=d_skip[None, :], w_glu=w_glu.astype(BF16))


def _tile(n, pref):
    for t in pref:
        if n % t == 0:
            return t
    return n


def _layer_tail(x_tm, layer, gains, ffn, conv_state, *, nb, tm, mixer=None):
    if mixer is not None:
        a, b, w = mixer
        mixer = (a, b, w["w_out_a"], w["w_out_b"], gains[layer, 1][None])
    return _ffn(x_tm, gains[layer, 2][None], ffn["w_up"][layer], ffn["conv_w"][layer], ffn["conv_b"][layer][None],
                ffn["w_down"][layer], gains[layer, 3][None], conv_state, nb=nb, tm=tm, mixer=mixer)


def _run_prompt(x, gains, w0, w1, ws5, ffn):
    nseq, seqlen, _ = x.shape
    rows = nseq * seqlen
    x2d = x.reshape(rows, D_MODEL)
    pos = jnp.arange(seqlen, dtype=jnp.int32)
    tabs = _rope_tables(pos, MLA_ROPE // 2, MLA_HEADS) + _rope_tables(pos, RET_DK // 2, RET_HEADS)
    tm = _tile(seqlen, (512, 256, 128))
    tq = _tile(seqlen, (256, 128))
    qlat, qpe, kfull, rows_new, qr, kr, vr, sg = _inproj0(x2d, gains[0, 0][None], w0, tabs, nseq=nseq,
                                                         seqlen=seqlen, tm=tm, idt=BF16)
    o_mla = _mla_prompt(qlat, qpe, kfull, w0["w_uv"], nseq=nseq, seqlen=seqlen, tq=tq)
    o_ret, s_new = _retention(qr, kr, vr, sg, jnp.zeros((nseq, RET_QK, RET_DV), F32), nseq=nseq, seqlen=seqlen,
                              odt=BF16)
    nt = seqlen // tm
    x1 = _outproj(x2d, o_mla, o_ret, w0["w_out_a"], w0["w_out_b"], gains[0, 1][None], grid=(nseq, nt),
                  x_map=lambda b, i: (b * nt + i, 0), ab_map=lambda b, i: (i, b), y_map=lambda b, i: (i, b),
                  y_shape=(seqlen, nseq * D_MODEL), tm=tm).reshape(rows, D_MODEL)
    tmf = _tile(rows, (512, 256, 128, 64))
    zero_state = jnp.zeros((2 * nseq, 2 * D_FF), F32)
    x2, cs0 = _layer_tail(x1, 0, gains, ffn, zero_state, nb=nseq, tm=tmf)
    u, q, k32, v32, kb, vb = _inproj1(x2, gains[1, 0][None], w1["w_in1"], tm=tmf, idt=BF16)
    tc = _tile(seqlen, (16, 8))
    zero_h = jnp.zeros((nseq, S5_HID), F32)
    o_s5, h_re, h_im = _s5(u.reshape(seqlen, nseq, S5_CH), zero_h, zero_h, ws5, tc=tc, bb=nseq, odt=BF16)
    tm2 = lambda a: a.reshape(seqlen, nseq * SB_W)
    o_sb, k_out, v_out = _sb_prompt(tm2(q), tm2(kb), tm2(vb), tm2(k32), tm2(v32), nseq=nseq, seqlen=seqlen, tq=tq)
    x4, cs1 = _layer_tail(x2, 1, gains, ffn, zero_state, nb=nseq, tm=tmf,
                          mixer=(o_s5.reshape(rows, S5_CH), o_sb.reshape(rows, SB_W), w1))
    y = _to_seq_major(x4.reshape(seqlen, nseq * D_MODEL), nseq=nseq, seqlen=seqlen, tm=tm)
    conv = jnp.stack([cs0, cs1]).reshape(2, CONV_W - 1, nseq, 2 * D_FF).transpose(0, 2, 1, 3)
    return (y.reshape(nseq, seqlen, D_MODEL), rows_new.reshape(nseq, seqlen, MLA_ROW),
            s_new.reshape(nseq, RET_HEADS, RET_DK, RET_DV), h_re.reshape(nseq, S5_GROUPS, S5_STATE),
            h_im.reshape(nseq, S5_GROUPS, S5_STATE), k_out.reshape(nseq, seqlen, SB_HEADS, SB_DIM),
            v_out.reshape(nseq, seqlen, SB_HEADS, SB_DIM), conv)


def _run_sample(x, gains, w0, w1, ws5, ffn, cache_mla, state_ret, s5_re, s5_im, cache_k, cache_v, state_conv,
                page_table):
    nseq, lnew, _ = x.shape
    rows = nseq * lnew
    past_len = page_table.shape[1] * PAGE_SIZE
    to_tm = lambda a: jnp.swapaxes(a, 0, 1)
    x_tm = to_tm(x).reshape(rows, D_MODEL)
    pos = past_len + jnp.arange(lnew, dtype=jnp.int32)
    pos_rows = jnp.repeat(pos, nseq)
    tabs = _rope_tables(pos_rows, MLA_ROPE // 2, MLA_HEADS) + _rope_tables(pos_rows, RET_DK // 2, RET_HEADS)
    tm = _tile(rows, (512, 256, 128, 64))
    qlat, qpe, _, rows_new, qr, kr, vr, sg = _inproj0(x_tm, gains[0, 0][None], w0, tabs, nseq=1, seqlen=rows,
                                                     tm=tm, idt=F32)
    ql = qlat.reshape(MLA_HEADS, lnew, nseq, MLA_KV_RANK).transpose(2, 0, 1, 3)
    qp = qpe.reshape(lnew, nseq, MLA_HEADS, MLA_ROPE).transpose(1, 2, 0, 3)
    qs = jnp.concatenate([ql, qp], axis=-1).reshape(nseq, MLA_HEADS * lnew, MLA_ROW)
    rows_seq = to_tm(rows_new.reshape(lnew, nseq, MLA_ROW))
    pad_keys = lambda a: jnp.pad(jnp.swapaxes(a, 1, 2), ((0, 0), (0, 0), (0, PAGE_SIZE - lnew)))
    npage = _tile(page_table.shape[1], (32, 16, 8, 4, 2))
    o_mla = _mla_sample(page_table, qs, pad_keys(rows_seq), w0["w_uv"], jnp.transpose(cache_mla, (0, 2, 1)),
                        npage=npage)
    col = lambda a, w: a.reshape(lnew, nseq * w)
    o_ret, s_new = _retention(col(qr, RET_QK), col(kr, RET_QK), col(vr, RET_W), col(sg, RET_W),
                              state_ret.reshape(nseq, RET_QK, RET_DV), nseq=nseq, seqlen=lnew, odt=F32)
    conv_tm = lambda st: jnp.swapaxes(st, 0, 1).reshape(2 * nseq, 2 * D_FF)
    tmf = max(2 * nseq, _tile(rows, (512, 256, 128)))
    x2, cs0 = _layer_tail(x_tm, 0, gains, ffn, conv_tm(state_conv[0]), nb=nseq, tm=tmf,
                          mixer=(to_tm(o_mla).reshape(rows, MLA_W), o_ret.reshape(rows, RET_W), w0))
    u, q, k32, v32, _, _ = _inproj1(x2, gains[1, 0][None], w1["w_in1"], tm=tm, idt=F32)
    bb = _tile(nseq, (64, 32, 16, 8))
    o_s5, h_re, h_im = _s5(u.reshape(lnew, nseq, S5_CH), s5_re.reshape(nseq, S5_HID), s5_im.reshape(nseq, S5_HID),
                           ws5, tc=lnew, bb=bb, odt=F32)
    seq = lambda a: to_tm(a.reshape(lnew, nseq, SB_W))
    k_seq, v_seq = seq(k32), seq(v32)
    q_hd = seq(q).reshape(nseq, lnew, SB_HEADS, SB_DIM).transpose(0, 2, 1, 3)
    q_bd = jnp.einsum('bhtd,hg->bhtgd', q_hd, jnp.eye(SB_HEADS, dtype=F32)).reshape(nseq, SB_HEADS * lnew, SB_W)
    npage_sb = _tile(page_table.shape[1], (32, 16, 8, 4, 2))
    o_sb = _sb_sample(page_table, q_bd, pad_keys(k_seq), pad_keys(v_seq), jnp.transpose(cache_k, (0, 2, 3, 1)),
                      jnp.transpose(cache_v, (0, 2, 3, 1)), npage=npage_sb)
    o_sb_tm = to_tm(o_sb).reshape(rows, SB_W)
    x4, cs1 = _layer_tail(x2, 1, gains, ffn, conv_tm(state_conv[1]), nb=nseq, tm=tmf,
                          mixer=(o_s5.reshape(rows, S5_CH), o_sb_tm, w1))
    y = to_tm(x4.reshape(lnew, nseq, D_MODEL))
    conv = jnp.stack([cs0, cs1]).reshape(2, CONV_W - 1, nseq, 2 * D_FF).transpose(0, 2, 1, 3)
    return (y, rows_seq, s_new.reshape(nseq, RET_HEADS, RET_DK, RET_DV), h_re.reshape(nseq, S5_GROUPS, S5_STATE),
            h_im.reshape(nseq, S5_GROUPS, S5_STATE), k_seq.reshape(nseq, lnew, SB_HEADS, SB_DIM),
            v_seq.reshape(nseq, lnew, SB_HEADS, SB_DIM), conv)


def kernel(x_prompt, x_sample, cache_mla, state_ret, state_s5_re, state_s5_im, cache_sb_k, cache_sb_v,
           state_ffn_conv, page_table, norm_gains, w_in_l0, mla_q_norm, mla_w_uq, mla_kv_norm, mla_w_uk,
           mla_w_uv, w_out_l0, w_in_l1, s5_lam_re, s5_lam_im, s5_log_dt, s5_b_re, s5_b_im, s5_c_re, s5_c_im,
           s5_d, s5_w_glu, w_out_l1, ffn_w_up, ffn_conv_w, ffn_conv_b, ffn_w_down):
    w0 = _prep_layer0(w_in_l0, mla_q_norm, mla_w_uq, mla_kv_norm, mla_w_uk, mla_w_uv, w_out_l0)
    w1 = dict(w_in1=w_in_l1.astype(BF16), w_out_a=w_out_l1[:S5_CH].astype(BF16),
              w_out_b=w_out_l1[S5_CH:].astype(BF16))
    ws5 = _prep_s5(s5_lam_re, s5_lam_im, s5_log_dt, s5_b_re, s5_b_im, s5_c_re, s5_c_im, s5_d, s5_w_glu)
    ffn = dict(w_up=ffn_w_up.astype(BF16), conv_w=ffn_conv_w, conv_b=ffn_conv_b, w_down=ffn_w_down.astype(BF16))
    gains = norm_gains.astype(F32)
    (y_p, mla_p, ret_p, s5r_p, s5i_p, sbk_p, sbv_p, conv_p) = _run_prompt(x_prompt, gains, w0, w1, ws5, ffn)
    (y_s, mla_s, ret_s, s5r_s, s5i_s, sbk_s, sbv_s, conv_s) = _run_sample(
        x_sample, gains, w0, w1, ws5, ffn, cache_mla, state_ret, state_s5_re, state_s5_im, cache_sb_k, cache_sb_v,
        state_ffn_conv, page_table)
    return (y_p, y_s, mla_p, mla_s, ret_p, ret_s, s5r_p, s5i_p, s5r_s, s5i_s, sbk_p, sbv_p, sbk_s, sbv_s,
            conv_p, conv_s)
```

```python
import functools
import math

import jax
import jax.numpy as jnp
import numpy as np
from jax import lax
from jax.experimental import pallas as pl
from jax.experimental.pallas import tpu as pltpu

F32 = jnp.float32
BF16 = jnp.bfloat16

D_MODEL = 1024
PAGE_SIZE = 128
MLA_HEADS = 8
MLA_Q_RANK = 384
MLA_KV_RANK = 256
MLA_NOPE = 64
MLA_ROPE = 32
MLA_V = 64
MLA_ROW = MLA_KV_RANK + MLA_ROPE
MLA_W = MLA_HEADS * MLA_V
RET_HEADS = 4
RET_DK = 64
RET_DV = 128
RET_W = RET_HEADS * RET_DV
RET_QK = RET_HEADS * RET_DK
RET_CHUNK = 128
S5_GROUP = 16
S5_CH = 640
S5_GROUPS = S5_CH // S5_GROUP
S5_STATE = 64
S5_HID = S5_GROUPS * S5_STATE
SB_HEADS = 6
SB_DIM = 64
SB_W = SB_HEADS * SB_DIM
D_FF = 2816
CONV_W = 3
ROPE_THETA = 10000.0
EPS = 1e-6
NEG = -1e30

VMEM_LIMIT_BYTES = 56 * 1024 * 1024
LANES = 128


def _cparams(sem):
    return pltpu.CompilerParams(dimension_semantics=sem, vmem_limit_bytes=VMEM_LIMIT_BYTES)


def _const_spec(shape):
    nd = len(shape)
    return pl.BlockSpec(shape, lambda *_: (0,) * nd, pipeline_mode=pl.Buffered(1))


def _rms(x):
    return x * lax.rsqrt(jnp.mean(x * x, axis=-1, keepdims=True) + EPS)


def _gelu_tanh(x):
    c = math.sqrt(2.0 / math.pi)
    return x * (0.5 * (1.0 + jnp.tanh(c * (x + 0.044715 * (x * x * x)))))


def _dot(a, b):
    return jnp.dot(a, b, preferred_element_type=F32)


def _dot_nt(a, b):
    return lax.dot_general(a, b, (((1,), (1,)), ((), ())), preferred_element_type=F32)


def _lane_band(x, lo, width):
    lane = lax.broadcasted_iota(jnp.int32, x.shape, x.ndim - 1)
    return jnp.where((lane >= lo) & (lane < lo + width), x, jnp.zeros_like(x))


_O_CQ, _O_CKV, _O_KPE, _O_KPES = 0, 384, 640, 896
_O_QR, _O_QRS, _O_KR, _O_KRS, _O_VR, _O_GR = 1152, 1408, 1664, 1920, 2176, 2688
IN0_AUG = 3200
LOG2E = math.log2(math.e)
_MLA_SCALE = (MLA_NOPE + MLA_ROPE) ** -0.5 * LOG2E


def _inproj0_kernel(x_ref, g_ref, win_ref, qn_ref, wuq_ref, kvn_ref, wuk_ref, cm_ref, sm_ref, cr_ref, sr_ref,
                    qlat_ref, qpe_ref, kfull_ref, rows_ref, qr_ref, kr_ref, vr_ref, sg_ref):
    xn = (_rms(x_ref[...]) * g_ref[...]).astype(BF16)
    z = _dot(xn, win_ref[...])
    cqn = (_rms(z[:, _O_CQ:_O_CQ + MLA_Q_RANK]) * qn_ref[...]).astype(BF16)
    q = _dot(cqn, wuq_ref[...])
    for p in range(MLA_HEADS // 2):
        ql = _dot(q[:, 128 * p:128 * (p + 1)].astype(BF16), wuk_ref[p]) * _MLA_SCALE
        qlat_ref[2 * p] = ql[:, :MLA_KV_RANK].astype(qlat_ref.dtype)
        qlat_ref[2 * p + 1] = ql[:, MLA_KV_RANK:].astype(qlat_ref.dtype)
    cm, sm = cm_ref[...], sm_ref[...]
    qpe = (q[:, 512:768] * cm + q[:, 768:1024] * sm) * _MLA_SCALE
    qpe_ref[...] = qpe.astype(qpe_ref.dtype)
    ckv = _rms(z[:, _O_CKV:_O_CKV + MLA_KV_RANK]) * kvn_ref[...]
    kpe = z[:, _O_KPE:_O_KPE + 256] * cm + z[:, _O_KPES:_O_KPES + 256] * sm
    kfull_ref[:, 0:256] = ckv.astype(kfull_ref.dtype)
    kfull_ref[:, 256:512] = kpe.astype(kfull_ref.dtype)
    rows_ref[:, 0:MLA_KV_RANK] = ckv
    rows_ref[:, MLA_KV_RANK:MLA_ROW] = kpe[:, 0:MLA_ROPE]
    cr, sr = cr_ref[...], sr_ref[...]
    qr_ref[...] = z[:, _O_QR:_O_QR + RET_QK] * cr + z[:, _O_QRS:_O_QRS + RET_QK] * sr
    kr_ref[...] = (z[:, _O_KR:_O_KR + RET_QK] * cr + z[:, _O_KRS:_O_KRS + RET_QK] * sr) * (RET_DK ** -0.5)
    vr_ref[...] = z[:, _O_VR:_O_VR + RET_W].astype(vr_ref.dtype)
    g = z[:, _O_GR:_O_GR + RET_W]
    sg_ref[...] = g * jax.nn.sigmoid(g)


def _inproj0(x2d, gain, w, tabs, *, nseq, seqlen, tm, idt):
    nt = seqlen // tm
    cm, sm, cr, sr = tabs
    row_in = lambda b, i: (b * nt + i, 0)
    tm_out = lambda b, i: (i, b)
    tab = lambda b, i: (i, 0)
    out_shape = (
        jax.ShapeDtypeStruct((MLA_HEADS, seqlen, nseq * 256), idt),
        jax.ShapeDtypeStruct((seqlen, nseq * 256), idt),
        jax.ShapeDtypeStruct((seqlen, nseq * 512), idt),
        jax.ShapeDtypeStruct((nseq * seqlen, MLA_ROW), F32),
        jax.ShapeDtypeStruct((seqlen, nseq * RET_QK), F32),
        jax.ShapeDtypeStruct((seqlen, nseq * RET_QK), F32),
        jax.ShapeDtypeStruct((seqlen, nseq * RET_W), idt),
        jax.ShapeDtypeStruct((seqlen, nseq * RET_W), F32),
    )
    out_specs = (
        pl.BlockSpec((MLA_HEADS, tm, 256), lambda b, i: (0, i, b)),
        pl.BlockSpec((tm, 256), tm_out),
        pl.BlockSpec((tm, 512), tm_out),
        pl.BlockSpec((tm, MLA_ROW), row_in),
        pl.BlockSpec((tm, RET_QK), tm_out),
        pl.BlockSpec((tm, RET_QK), tm_out),
        pl.BlockSpec((tm, RET_W), tm_out),
        pl.BlockSpec((tm, RET_W), tm_out),
    )
    in_specs = [
        pl.BlockSpec((tm, D_MODEL), row_in),
        _const_spec((1, D_MODEL)),
        _const_spec((D_MODEL, IN0_AUG)),
        _const_spec((1, MLA_Q_RANK)),
        _const_spec((MLA_Q_RANK, 1024)),
        _const_spec((1, MLA_KV_RANK)),
        _const_spec((MLA_HEADS // 2, 128, 512)),
        pl.BlockSpec((tm, 256), tab), pl.BlockSpec((tm, 256), tab),
        pl.BlockSpec((tm, 256), tab), pl.BlockSpec((tm, 256), tab),
    ]
    return pl.pallas_call(
        _inproj0_kernel, grid=(nseq, nt), in_specs=in_specs, out_specs=out_specs, out_shape=out_shape,
        compiler_params=_cparams(("parallel", "parallel")), name="inproj0",
    )(x2d, gain, w["w_in0"], w["q_norm"], w["w_uq"], w["kv_norm"], w["w_uk"], cm, sm, cr, sr)


def _mla_value_proj(o_lat, wuv_ref, n):
    pairs = []
    for p in range(MLA_HEADS // 2):
        pairs.append(_dot(o_lat[2 * p * n:(2 * p + 1) * n], wuv_ref[2 * p])
                     + _dot(o_lat[(2 * p + 1) * n:(2 * p + 2) * n], wuv_ref[2 * p + 1]))
    return jnp.concatenate(pairs, axis=1)


def _mla_prompt_kernel(qlat_ref, qpe_ref, k_ref, wuv_ref, o_ref, qs_ref, s_ref, mx_ref, ls_ref, acc_ref, *, tq):
    qi = pl.program_id(1)
    rows = MLA_HEADS * tq
    half = tq // 2
    assert half == LANES
    qpe = qpe_ref[...]
    for h in range(MLA_HEADS):
        qs_ref[h * tq:(h + 1) * tq, 0:256] = qlat_ref[h]
        qs_ref[h * tq:(h + 1) * tq, 256:512] = _lane_band(qpe, MLA_ROPE * h, MLA_ROPE)
    mx_ref[...] = jnp.full((rows, LANES), NEG, F32)
    ls_ref[...] = jnp.zeros((rows, LANES), F32)
    acc_ref[...] = jnp.zeros((rows, MLA_KV_RANK), F32)

    def kblock(j):
        return k_ref[pl.ds(pl.multiple_of(j * tq, tq), tq), :]

    def pass1(j, c):
        s = _dot_nt(qs_ref[...], kblock(j))
        s_ref[j] = s
        mx_ref[...] = jnp.maximum(mx_ref[...], jnp.maximum(s[:, :half], s[:, half:]))
        return c

    lax.fori_loop(0, qi, pass1, 0)
    t_q = lax.broadcasted_iota(jnp.int32, (rows, tq), 0) & (tq - 1)
    t_k = lax.broadcasted_iota(jnp.int32, (rows, tq), 1)
    s = jnp.where(t_k <= t_q, _dot_nt(qs_ref[...], kblock(qi)), NEG)
    s_ref[qi] = s
    mx = jnp.maximum(mx_ref[...], jnp.maximum(s[:, :half], s[:, half:]))
    mx_ref[...] = jnp.broadcast_to(jnp.max(mx, axis=-1, keepdims=True), (rows, LANES))

    def pass2(j, c):
        s = s_ref[j]
        m = mx_ref[...]
        p0 = jnp.exp2(s[:, :half] - m)
        p1 = jnp.exp2(s[:, half:] - m)
        ls_ref[...] += p0 + p1
        p = jnp.concatenate([p0, p1], axis=1).astype(BF16)
        acc_ref[...] += _dot(p, kblock(j)[:, 0:MLA_KV_RANK])
        return c

    lax.fori_loop(0, qi + 1, pass2, 0)
    l = jnp.sum(ls_ref[...], axis=-1, keepdims=True)
    o_lat = (acc_ref[...] / l).astype(BF16)
    o_ref[...] = _mla_value_proj(o_lat, wuv_ref, tq).astype(o_ref.dtype)


def _mla_prompt(qlat, qpe, kfull, wuv, *, nseq, seqlen, tq):
    nq = seqlen // tq
    rows = MLA_HEADS * tq
    assert tq & (tq - 1) == 0
    return pl.pallas_call(
        functools.partial(_mla_prompt_kernel, tq=tq),
        grid=(nseq, nq),
        in_specs=[
            pl.BlockSpec((MLA_HEADS, tq, 256), lambda b, i: (0, i, b)),
            pl.BlockSpec((tq, 256), lambda b, i: (i, b)),
            pl.BlockSpec((seqlen, 512), lambda b, i: (0, b)),
            _const_spec((MLA_HEADS, MLA_KV_RANK, 2 * MLA_V)),
        ],
        out_specs=pl.BlockSpec((tq, MLA_W), lambda b, i: (i, b)),
        out_shape=jax.ShapeDtypeStruct((seqlen, nseq * MLA_W), BF16),
        scratch_shapes=[pltpu.VMEM((rows, 512), BF16), pltpu.VMEM((nq, rows, tq), F32),
                        pltpu.VMEM((rows, LANES), F32), pltpu.VMEM((rows, LANES), F32),
                        pltpu.VMEM((rows, MLA_KV_RANK), F32)],
        compiler_params=_cparams(("parallel", "parallel")), name="mla_prompt",
    )(qlat, qpe, kfull, wuv)


def _mla_sample_kernel(pt_ref, qs_ref, knt_ref, wuv_ref, *refs, npage, nstep, lnew):
    pages = refs[:npage]
    o_ref = refs[npage]
    m_ref, l_ref, acc_ref = refs[npage + 1:]
    step = pl.program_id(1)
    rows = MLA_HEADS * lnew

    @pl.when(step == 0)
    def _():
        m_ref[...] = jnp.full((rows, 1), NEG, F32)
        l_ref[...] = jnp.zeros((rows, 1), F32)
        acc_ref[...] = jnp.zeros((rows, MLA_KV_RANK), F32)

    q = qs_ref[0].astype(BF16)

    def update(s, kts):
        m_prev = m_ref[...]
        m_new = jnp.maximum(m_prev, jnp.max(s, axis=-1, keepdims=True))
        alpha = jnp.exp2(m_prev - m_new)
        p = jnp.exp2(s - m_new)
        l_ref[...] = alpha * l_ref[...] + jnp.sum(p, axis=-1, keepdims=True)
        pv = _dot_nt(p[:, 0:PAGE_SIZE].astype(BF16), kts[0][0:MLA_KV_RANK])
        for n in range(1, len(kts)):
            pv = pv + _dot_nt(p[:, n * PAGE_SIZE:(n + 1) * PAGE_SIZE].astype(BF16), kts[n][0:MLA_KV_RANK])
        acc_ref[...] = alpha * acc_ref[...] + pv
        m_ref[...] = m_new

    kts = [pages[n][0].astype(BF16) for n in range(npage)]
    update(jnp.concatenate([_dot(q, kt) for kt in kts], axis=1), kts)

    @pl.when(step == nstep - 1)
    def _():
        knt = knt_ref[0].astype(BF16)
        sn = _dot(q, knt)
        t_q = lax.broadcasted_iota(jnp.int32, (rows, PAGE_SIZE), 0) & (lnew - 1)
        t_k = lax.broadcasted_iota(jnp.int32, (rows, PAGE_SIZE), 1)
        update(jnp.where(t_k <= t_q, sn, NEG), [knt])
        o_lat = (acc_ref[...] / l_ref[...]).astype(BF16)
        o_ref[0] = _mla_value_proj(o_lat, wuv_ref, lnew)


def _mla_sample(page_table, qs, knew_t, wuv, cache_t, *, npage):
    nseq, n_pages = page_table.shape
    rows = qs.shape[1]
    lnew = rows // MLA_HEADS
    assert lnew & (lnew - 1) == 0 and lnew <= PAGE_SIZE
    nstep = n_pages // npage
    page_specs = [
        pl.BlockSpec((1, MLA_ROW, PAGE_SIZE), lambda b, s, pt, n=n: (pt[b, s * npage + n], 0, 0))
        for n in range(npage)
    ]
    grid_spec = pltpu.PrefetchScalarGridSpec(
        num_scalar_prefetch=1, grid=(nseq, nstep),
        in_specs=[
            pl.BlockSpec((1, rows, MLA_ROW), lambda b, s, pt: (b, 0, 0)),
            pl.BlockSpec((1, MLA_ROW, PAGE_SIZE), lambda b, s, pt: (b, 0, 0)),
            pl.BlockSpec((MLA_HEADS, MLA_KV_RANK, 2 * MLA_V), lambda b, s, pt: (0, 0, 0)),
        ] + page_specs,
        out_specs=pl.BlockSpec((1, lnew, MLA_W), lambda b, s, pt: (b, 0, 0)),
        scratch_shapes=[pltpu.VMEM((rows, 1), F32), pltpu.VMEM((rows, 1), F32),
                        pltpu.VMEM((rows, MLA_KV_RANK), F32)],
    )
    return pl.pallas_call(
        functools.partial(_mla_sample_kernel, npage=npage, nstep=nstep, lnew=lnew),
        grid_spec=grid_spec,
        out_shape=jax.ShapeDtypeStruct((nseq, lnew, MLA_W), F32),
        compiler_params=_cparams(("parallel", "arbitrary")), name="mla_sample",
    )(page_table, qs, knew_t, wuv, *([cache_t] * npage))


def _retention_kernel(q_ref, k_ref, v_ref, sg_ref, intra_ref, qdec_ref, kdec_ref, cdec_ref, s0_ref,
                      o_ref, so_ref, s_ref, pad_ref, *, c, nchunk, lrows):
    i = pl.program_id(1)

    @pl.when(i == 0)
    def _():
        s_ref[...] = s0_ref[0]

    row = lax.broadcasted_iota(jnp.int32, (RET_QK, RET_DV), 0)
    for ch in range(nchunk):
        if lrows < c:
            pad_ref[...] = jnp.zeros(pad_ref.shape, F32)
            pad_ref[0:lrows, 0:RET_QK] = q_ref[...]
            pad_ref[0:lrows, RET_QK:2 * RET_QK] = k_ref[...]
            pad_ref[0:lrows, 2 * RET_QK:2 * RET_QK + RET_W] = v_ref[...].astype(F32)
            q = pad_ref[:, 0:RET_QK]
            k = pad_ref[:, RET_QK:2 * RET_QK]
            v_all = pad_ref[:, 2 * RET_QK:2 * RET_QK + RET_W].astype(BF16)
        else:
            rs = slice(ch * c, (ch + 1) * c)
            q, k, v_all = q_ref[rs, :], k_ref[rs, :], v_ref[rs, :].astype(BF16)
        qd = q * qdec_ref[...]
        kdt = (k * kdec_ref[...]).T.astype(BF16)
        kb = k.astype(BF16)
        s_old = s_ref[...]
        sb = s_old.astype(BF16)
        s_new = cdec_ref[...] * s_old
        for h in range(RET_HEADS):
            qh = _lane_band(q, RET_DK * h, RET_DK).astype(BF16)
            qdh = _lane_band(qd, RET_DK * h, RET_DK).astype(BF16)
            vh = v_all[:, RET_DV * h:RET_DV * (h + 1)]
            att = _dot_nt(qh, kb) * intra_ref[h]
            o = _dot(att.astype(BF16), vh) + _dot(qdh, sb)
            o = _rms(o)
            if lrows < c:
                o_ref[:, RET_DV * h:RET_DV * (h + 1)] = (
                    o[0:lrows] * sg_ref[:, RET_DV * h:RET_DV * (h + 1)]).astype(o_ref.dtype)
            else:
                o_ref[rs, RET_DV * h:RET_DV * (h + 1)] = (
                    o * sg_ref[rs, RET_DV * h:RET_DV * (h + 1)]).astype(o_ref.dtype)
            upd = _dot(kdt, vh)
            s_new = s_new + jnp.where((row >= RET_DK * h) & (row < RET_DK * (h + 1)), upd, 0.0)
        s_ref[...] = s_new

    @pl.when(i == pl.num_programs(1) - 1)
    def _():
        so_ref[0] = s_ref[...]


def _retention_tables(c_eff, c_pad):
    log_g = jnp.log(1.0 - 2.0 ** (-5.0 - jnp.arange(RET_HEADS, dtype=F32)))
    i = jnp.arange(c_pad, dtype=F32)
    diff = i[:, None] - i[None, :]
    intra = jnp.where(diff >= 0, jnp.exp(jnp.maximum(diff, 0.0)[None] * log_g[:, None, None]), 0.0)
    q_dec = jnp.exp((i[:, None] + 1.0) * log_g[None, :])
    k_dec = jnp.exp(jnp.maximum(c_eff - 1.0 - i, 0.0)[:, None] * log_g[None, :])
    c_dec = jnp.exp(c_eff * log_g)
    rep = lambda a: jnp.repeat(a, RET_DK, axis=1)
    cdec_rows = jnp.broadcast_to(jnp.repeat(c_dec, RET_DK)[:, None], (RET_QK, RET_DV))
    return intra.astype(F32), rep(q_dec), rep(k_dec), cdec_rows


def _retention(qr, kr, vr, sg, s0, *, nseq, seqlen, odt):
    c = RET_CHUNK
    if seqlen % c == 0:
        c_eff, tr = c, _tile(seqlen, (8 * c, 4 * c, 2 * c, c))
        nchunk, lrows = tr // c, tr
    else:
        c_eff, tr, nchunk, lrows = seqlen, seqlen, 1, seqlen
    nt = seqlen // tr
    intra, qdec, kdec, cdec = _retention_tables(c_eff, c)
    blk = lambda w: pl.BlockSpec((tr, w), lambda b, i: (i, b))
    return pl.pallas_call(
        functools.partial(_retention_kernel, c=c, nchunk=nchunk, lrows=lrows),
        grid=(nseq, nt),
        in_specs=[blk(RET_QK), blk(RET_QK), blk(RET_W), blk(RET_W),
                  _const_spec((RET_HEADS, c, c)), _const_spec((c, RET_QK)), _const_spec((c, RET_QK)),
                  _const_spec((RET_QK, RET_DV)),
                  pl.BlockSpec((1, RET_QK, RET_DV), lambda b, i: (b, 0, 0))],
        out_specs=(blk(RET_W), pl.BlockSpec((1, RET_QK, RET_DV), lambda b, i: (b, 0, 0))),
        out_shape=(jax.ShapeDtypeStruct((seqlen, nseq * RET_W), odt),
                   jax.ShapeDtypeStruct((nseq, RET_QK, RET_DV), F32)),
        scratch_shapes=[pltpu.VMEM((RET_QK, RET_DV), F32), pltpu.VMEM((c, 2 * RET_QK + RET_W), F32)],
        compiler_params=_cparams(("parallel", "arbitrary")), name="retention",
    )(qr, kr, vr, sg, intra, qdec, kdec, cdec, s0)


def _outproj_kernel(x_ref, a_ref, b_ref, wa_ref, wb_ref, g_ref, y_ref):
    m = _dot(a_ref[...].astype(BF16), wa_ref[...]) + _dot(b_ref[...].astype(BF16), wb_ref[...])
    y_ref[...] = x_ref[...] + _rms(m) * g_ref[...]


def _outproj(x, a, b, wa, wb, gain, *, grid, x_map, ab_map, y_map, y_shape, tm):
    wa_w, wb_w = wa.shape[0], wb.shape[0]
    return pl.pallas_call(
        _outproj_kernel, grid=grid,
        in_specs=[pl.BlockSpec((tm, D_MODEL), x_map), pl.BlockSpec((tm, wa_w), ab_map),
                  pl.BlockSpec((tm, wb_w), ab_map), _const_spec(wa.shape), _const_spec(wb.shape),
                  _const_spec((1, D_MODEL))],
        out_specs=pl.BlockSpec((tm, D_MODEL), y_map),
        out_shape=jax.ShapeDtypeStruct(y_shape, F32),
        compiler_params=_cparams(("parallel",) * len(grid)), name="outproj",
    )(x, a, b, wa, wb, gain)


_FFN_CK = 256


def _ffn_kernel(*refs, nb, tm, mixer):
    if mixer:
        x_ref, a_ref, b_ref, wa_ref, wb_ref, g1_ref = refs[:6]
        refs = refs[6:]
    else:
        x_ref = refs[0]
        refs = refs[1:]
    g2_ref, wup_ref, cw_ref, cb_ref, wdn_ref, g3_ref, st_ref, y_ref, cs_ref, carry_ref = refs
    i = pl.program_id(0)
    ck = _FFN_CK

    @pl.when(i == 0)
    def _():
        carry_ref[...] = st_ref[...]

    x = x_ref[...]
    if mixer:
        m = _dot(a_ref[...].astype(BF16), wa_ref[...]) + _dot(b_ref[...].astype(BF16), wb_ref[...])
        x = x + _rms(m) * g1_ref[...]
    xn = (_rms(x) * g2_ref[...]).astype(BF16)
    acc = jnp.zeros((tm, D_MODEL), F32)
    for c in range(D_FF // ck):
        act = []
        for half in range(2):
            col = half * D_FF + c * ck
            cs = slice(col, col + ck)
            h = _dot(xn, wup_ref[:, cs])
            prev2, prev1 = carry_ref[0:nb, cs], carry_ref[nb:2 * nb, cs]
            carry_ref[:, cs] = h[tm - 2 * nb:tm]
            hm2 = jnp.concatenate([prev2, prev1, h[0:tm - 2 * nb]], axis=0)
            hm1 = jnp.concatenate([prev1, h[0:tm - nb]], axis=0)
            act.append(cb_ref[:, cs] + cw_ref[0:1, cs] * hm2 + cw_ref[1:2, cs] * hm1 + cw_ref[2:3, cs] * h)
        a = (_gelu_tanh(act[0]) * act[1]).astype(BF16)
        acc = acc + _dot(a, wdn_ref[c * ck:(c + 1) * ck, :])
    y_ref[...] = x + _rms(acc) * g3_ref[...]

    @pl.when(i == pl.num_programs(0) - 1)
    def _():
        cs_ref[...] = carry_ref[...]


def _ffn(x, g2, wup, cw, cb, wdn, g3, state, *, nb, tm, mixer=None):
    rows = x.shape[0]
    assert tm >= 2 * nb and tm % nb == 0 and rows % tm == 0
    blk = lambda w: pl.BlockSpec((tm, w), lambda i: (i, 0))
    ins, specs = [x], [blk(D_MODEL)]
    if mixer is not None:
        a, b, wa, wb, g1 = mixer
        ins += [a, b, wa, wb, g1]
        specs += [blk(a.shape[1]), blk(b.shape[1]), _const_spec(wa.shape), _const_spec(wb.shape),
                  _const_spec((1, D_MODEL))]
    ins += [g2, wup, cw, cb, wdn, g3, state]
    specs += [_const_spec((1, D_MODEL)), _const_spec((D_MODEL, 2 * D_FF)), _const_spec((CONV_W, 2 * D_FF)),
              _const_spec((1, 2 * D_FF)), _const_spec((D_FF, D_MODEL)), _const_spec((1, D_MODEL)),
              _const_spec((2 * nb, 2 * D_FF))]
    return pl.pallas_call(
        functools.partial(_ffn_kernel, nb=nb, tm=tm, mixer=mixer is not None),
        grid=(rows // tm,),
        in_specs=specs,
        out_specs=(blk(D_MODEL), pl.BlockSpec((2 * nb, 2 * D_FF), lambda i: (0, 0))),
        out_shape=(jax.ShapeDtypeStruct((rows, D_MODEL), F32), jax.ShapeDtypeStruct((2 * nb, 2 * D_FF), F32)),
        scratch_shapes=[pltpu.VMEM((2 * nb, 2 * D_FF), F32)],
        compiler_params=_cparams(("arbitrary",)), name="convffn",
    )(*ins)


def _inproj1_kernel(x_ref, g_ref, win_ref, u_ref, q_ref, k_ref, v_ref, kb_ref, vb_ref):
    xn = (_rms(x_ref[...]) * g_ref[...]).astype(BF16)
    z = _dot(xn, win_ref[...])
    u_ref[...] = z[:, 0:S5_CH]
    q_ref[...] = (z[:, S5_CH:S5_CH + SB_W] * (SB_DIM ** -0.5)).astype(q_ref.dtype)
    k = z[:, S5_CH + SB_W:S5_CH + 2 * SB_W]
    v = z[:, S5_CH + 2 * SB_W:S5_CH + 3 * SB_W]
    k_ref[...] = k
    v_ref[...] = v
    kb_ref[...] = k.astype(kb_ref.dtype)
    vb_ref[...] = v.astype(vb_ref.dtype)


def _inproj1(x, gain, win, *, tm, idt):
    rows = x.shape[0]
    blk = lambda w: pl.BlockSpec((tm, w), lambda i: (i, 0))
    sds = lambda w, dt: jax.ShapeDtypeStruct((rows, w), dt)
    return pl.pallas_call(
        _inproj1_kernel, grid=(rows // tm,),
        in_specs=[blk(D_MODEL), _const_spec((1, D_MODEL)), _const_spec(win.shape)],
        out_specs=(blk(S5_CH), blk(SB_W), blk(SB_W), blk(SB_W), blk(SB_W), blk(SB_W)),
        out_shape=(sds(S5_CH, F32), sds(SB_W, idt), sds(SB_W, F32), sds(SB_W, F32), sds(SB_W, idt), sds(SB_W, idt)),
        compiler_params=_cparams(("parallel",)), name="inproj1",
    )(x, gain, win)


_S5_LW = 256
_S5_CB = 128
_S5_NBLK = S5_CH // _S5_CB
_S5_HB = S5_HID // _S5_NBLK


def _s5_kernel(u_ref, h0r_ref, h0i_ref, br_ref, bi_ref, ar_ref, ai_ref, cr_ref, cin_ref, d_ref, wg_ref,
               o_ref, hr_ref, hi_ref, *, tc, bb):
    i = pl.program_id(1)

    @pl.when(i == 0)
    def _():
        hr_ref[...] = h0r_ref[...]
        hi_ref[...] = h0i_ref[...]

    u = u_ref[...].reshape(tc * bb, S5_CH)
    ub = u.astype(BF16)
    ys = []
    for m in range(_S5_NBLK):
        um = ub[:, _S5_CB * m:_S5_CB * (m + 1)]
        sr, si = _dot(um, br_ref[m]), _dot(um, bi_ref[m])
        hre, him = [], []
        for sub in range(_S5_HB // _S5_LW):
            ls = slice(sub * _S5_LW, (sub + 1) * _S5_LW)
            cs = slice(_S5_HB * m + sub * _S5_LW, _S5_HB * m + (sub + 1) * _S5_LW)
            ar, ai = ar_ref[:, cs], ai_ref[:, cs]
            hr, hi = hr_ref[:, cs], hi_ref[:, cs]
            hrs, his = [], []
            for t in range(tc):
                rs = slice(t * bb, (t + 1) * bb)
                hr, hi = ar * hr - ai * hi + sr[rs, ls], ar * hi + ai * hr + si[rs, ls]
                hrs.append(hr)
                his.append(hi)
            hr_ref[:, cs] = hr
            hi_ref[:, cs] = hi
            hre.append(jnp.concatenate(hrs, axis=0))
            him.append(jnp.concatenate(his, axis=0))
        ys.append(_dot(jnp.concatenate(hre, axis=1).astype(BF16), cr_ref[m])
                  + _dot(jnp.concatenate(him, axis=1).astype(BF16), cin_ref[m]))
    y = jnp.concatenate(ys, axis=1)
    y = _gelu_tanh(y + d_ref[...] * u)
    o = y * jax.nn.sigmoid(_dot(y.astype(BF16), wg_ref[...]))
    o_ref[...] = o.reshape(tc, bb, S5_CH).astype(o_ref.dtype)


def _s5(u3, h0r, h0i, w, *, tc, bb, odt):
    tlen, nseq, _ = u3.shape
    ublk = pl.BlockSpec((tc, bb, S5_CH), lambda j, i: (i, j, 0))
    hblk = pl.BlockSpec((bb, S5_HID), lambda j, i: (j, 0))
    return pl.pallas_call(
        functools.partial(_s5_kernel, tc=tc, bb=bb),
        grid=(nseq // bb, tlen // tc),
        in_specs=[ublk, hblk, hblk, _const_spec((_S5_NBLK, _S5_CB, _S5_HB)), _const_spec((_S5_NBLK, _S5_CB, _S5_HB)),
                  _const_spec((1, S5_HID)), _const_spec((1, S5_HID)), _const_spec((_S5_NBLK, _S5_HB, _S5_CB)),
                  _const_spec((_S5_NBLK, _S5_HB, _S5_CB)), _const_spec((1, S5_CH)), _const_spec((S5_CH, S5_CH))],
        out_specs=(ublk, hblk, hblk),
        out_shape=(jax.ShapeDtypeStruct((tlen, nseq, S5_CH), odt), jax.ShapeDtypeStruct((nseq, S5_HID), F32),
                   jax.ShapeDtypeStruct((nseq, S5_HID), F32)),
        compiler_params=_cparams(("parallel", "arbitrary")), name="s5",
    )(u3, h0r, h0i, w["bbr"], w["bbi"], w["abr"], w["abi"], w["ccr"], w["ccin"], w["d"], w["w_glu"])


def _sb_weights(z, tri, carry, valid):
    lsz, lk, cum = _sb_logs(z, tri, valid)
    w = jnp.exp(lsz + (cum + carry))
    if valid is not None:
        w = jnp.where(valid, w, 0.0)
    return w.astype(BF16), jnp.sum(lk, axis=-1, keepdims=True)


def _sb_logs(z, tri, valid):
    t = jnp.log(1.0 + jnp.exp2(jnp.abs(z) * (-LOG2E)))
    lsz = jnp.minimum(z, 0.0) - t
    lk = lsz - z
    if valid is not None:
        lk = jnp.where(valid, lk, 0.0)
    hi = lk.astype(BF16)
    lo = (lk - hi.astype(F32)).astype(BF16)
    return lsz, lk, _dot(jnp.concatenate([hi, lo], axis=1), tri)


def _sb_prompt_kernel(q_ref, kb_ref, vb_ref, k32_ref, v32_ref, tri_ref, o_ref, ko_ref, vo_ref,
                      qm_ref, carry_ref, acc_ref, *, tq):
    qi = pl.program_id(1)
    rows = SB_HEADS * tq
    ko_ref[...] = k32_ref[...]
    vo_ref[...] = v32_ref[...]
    tri = tri_ref[...]
    for h in range(SB_HEADS):
        qm_ref[h] = _lane_band(q_ref[:, LANES * (h // 2):LANES * (h // 2 + 1)], SB_DIM * (h % 2), SB_DIM)
    carry_ref[...] = jnp.zeros(carry_ref.shape, F32)
    acc_ref[...] = jnp.zeros(acc_ref.shape, F32)

    def visit(j, valid):
        rs = pl.ds(pl.multiple_of(j * tq, tq), tq)
        ps = [slice(LANES * (h // 2), LANES * (h // 2 + 1)) for h in range(SB_HEADS)]
        z = jnp.concatenate([_dot_nt(qm_ref[h], kb_ref[rs, ps[h]]) for h in range(SB_HEADS)], axis=0)
        w, lksum = _sb_weights(z, tri, carry_ref[...], valid)
        for h in range(SB_HEADS):
            acc_ref[h] += _dot(w[h * tq:(h + 1) * tq], vb_ref[rs, ps[h]])
        carry_ref[...] += lksum

    t_q = lax.broadcasted_iota(jnp.int32, (rows, tq), 0) & (tq - 1)
    t_k = lax.broadcasted_iota(jnp.int32, (rows, tq), 1)
    visit(qi, t_k < t_q)

    def body(jj, c):
        visit(qi - 1 - jj, None)
        return c

    lax.fori_loop(0, qi, body, 0)
    lane = lax.broadcasted_iota(jnp.int32, (tq, LANES), 1)
    for p in range(SB_HEADS // 2):
        o_ref[:, LANES * p:LANES * (p + 1)] = jnp.where(
            lane < SB_DIM, acc_ref[2 * p], acc_ref[2 * p + 1]).astype(o_ref.dtype)


def _tri(n):
    j = jnp.arange(n)
    t = (j[:, None] > j[None, :]).astype(BF16)
    return jnp.concatenate([t, t], axis=0)


def _sb_prompt(q, kb, vb, k32, v32, *, nseq, seqlen, tq):
    nq = seqlen // tq
    colblk = pl.BlockSpec((tq, SB_W), lambda b, i: (i, b))
    seqblk = pl.BlockSpec((seqlen, SB_W), lambda b, i: (0, b))
    rowblk = pl.BlockSpec((tq, SB_W), lambda b, i: (b * nq + i, 0))
    return pl.pallas_call(
        functools.partial(_sb_prompt_kernel, tq=tq),
        grid=(nseq, nq),
        in_specs=[colblk, seqblk, seqblk, colblk, colblk, _const_spec((2 * tq, tq))],
        out_specs=(colblk, rowblk, rowblk),
        out_shape=(jax.ShapeDtypeStruct((seqlen, nseq * SB_W), BF16),
                   jax.ShapeDtypeStruct((nseq * seqlen, SB_W), F32),
                   jax.ShapeDtypeStruct((nseq * seqlen, SB_W), F32)),
        scratch_shapes=[pltpu.VMEM((SB_HEADS, tq, LANES), BF16), pltpu.VMEM((SB_HEADS * tq, 1), F32),
                        pltpu.VMEM((SB_HEADS, tq, LANES), F32)],
        compiler_params=_cparams(("parallel", "parallel")), name="sb_prompt",
    )(q, kb, vb, k32, v32, _tri(tq))


def _sb_sample_kernel(pt_ref, q_ref, knt_ref, vnt_ref, tri_ref, *refs, npage, nstep, lnew):
    kpages = refs[:npage]
    vpages = refs[npage:2 * npage]
    o_ref = refs[2 * npage]
    carry_ref, acc_ref = refs[2 * npage + 1:]
    step = pl.program_id(1)
    rows = SB_HEADS * lnew
    tri = tri_ref[...]
    q = q_ref[0].astype(BF16)

    @pl.when(step == 0)
    def _():
        t_q = lax.broadcasted_iota(jnp.int32, (rows, PAGE_SIZE), 0) & (lnew - 1)
        t_k = lax.broadcasted_iota(jnp.int32, (rows, PAGE_SIZE), 1)
        w, lksum = _sb_weights(_dot(q, knt_ref[0].astype(BF16)), tri, jnp.zeros((rows, 1), F32), t_k < t_q)
        acc_ref[...] = _dot_nt(w, vnt_ref[0].astype(BF16))
        carry_ref[...] = lksum

    z = jnp.concatenate([_dot(q, kpages[n][0].reshape(SB_W, PAGE_SIZE).astype(BF16)) for n in range(npage)],
                        axis=0)
    lsz, lk, cum = _sb_logs(z, tri, None)
    lksum = jnp.sum(lk, axis=-1, keepdims=True)
    carries, carry = [], carry_ref[...]
    for n in range(npage):
        carries.append(carry)
        carry = carry + lksum[n * rows:(n + 1) * rows]
    carry_ref[...] = carry
    w = jnp.exp(lsz + cum + jnp.concatenate(carries, axis=0)).astype(BF16)
    pv = _dot_nt(w[0:rows], vpages[0][0].reshape(SB_W, PAGE_SIZE).astype(BF16))
    for n in range(1, npage):
        pv = pv + _dot_nt(w[n * rows:(n + 1) * rows], vpages[n][0].reshape(SB_W, PAGE_SIZE).astype(BF16))
    acc_ref[...] += pv

    @pl.when(step == nstep - 1)
    def _():
        acc = acc_ref[...]
        out = _lane_band(acc[0:lnew], 0, SB_DIM)
        for h in range(1, SB_HEADS):
            out = out + _lane_band(acc[h * lnew:(h + 1) * lnew], SB_DIM * h, SB_DIM)
        o_ref[0] = out


def _sb_sample(page_table, q_bd, knew_t, vnew_t, cache_kt, cache_vt, *, npage):
    nseq, n_pages = page_table.shape
    rows = q_bd.shape[1]
    lnew = rows // SB_HEADS
    assert lnew & (lnew - 1) == 0 and lnew <= PAGE_SIZE
    nstep = n_pages // npage
    pidx = lambda b, s, pt, n: (pt[b, n_pages - 1 - (s * npage + n)], 0, 0, 0)
    pspecs = [pl.BlockSpec((1, SB_HEADS, SB_DIM, PAGE_SIZE), functools.partial(pidx, n=n)) for n in range(npage)]
    grid_spec = pltpu.PrefetchScalarGridSpec(
        num_scalar_prefetch=1, grid=(nseq, nstep),
        in_specs=[
            pl.BlockSpec((1, rows, SB_W), lambda b, s, pt: (b, 0, 0)),
            pl.BlockSpec((1, SB_W, PAGE_SIZE), lambda b, s, pt: (b, 0, 0)),
            pl.BlockSpec((1, SB_W, PAGE_SIZE), lambda b, s, pt: (b, 0, 0)),
            pl.BlockSpec((2 * PAGE_SIZE, PAGE_SIZE), lambda b, s, pt: (0, 0)),
        ] + pspecs + pspecs,
        out_specs=pl.BlockSpec((1, lnew, SB_W), lambda b, s, pt: (b, 0, 0)),
        scratch_shapes=[pltpu.VMEM((rows, 1), F32), pltpu.VMEM((rows, SB_W), F32)],
    )
    return pl.pallas_call(
        functools.partial(_sb_sample_kernel, npage=npage, nstep=nstep, lnew=lnew),
        grid_spec=grid_spec,
        out_shape=jax.ShapeDtypeStruct((nseq, lnew, SB_W), F32),
        compiler_params=_cparams(("parallel", "arbitrary")), name="sb_sample",
    )(page_table, q_bd, knew_t, vnew_t, _tri(PAGE_SIZE), *([cache_kt] * npage), *([cache_vt] * npage))


def _swap_halves(n_heads, dim):
    idx = np.arange(n_heads * dim)
    return (idx // dim) * dim + (idx % dim + dim // 2) % dim


def _rope_tables(pos, half, reps):
    freqs = ROPE_THETA ** (-jnp.arange(half, dtype=F32) / half)
    ang = pos.astype(F32)[:, None] * freqs[None, :]
    c, s = jnp.cos(ang), jnp.sin(ang)
    return (jnp.tile(jnp.concatenate([c, c], axis=-1), (1, reps)),
            jnp.tile(jnp.concatenate([-s, s], axis=-1), (1, reps)))


def _prep_layer0(w_in, q_norm, w_uq, kv_norm, w_uk, w_uv, w_out):
    o_kpe = MLA_Q_RANK + MLA_KV_RANK
    o_qr = o_kpe + MLA_ROPE
    o_kr = o_qr + RET_QK
    o_v = o_kr + RET_QK
    cols = np.concatenate([
        np.arange(0, o_kpe),
        np.tile(o_kpe + np.arange(MLA_ROPE), MLA_HEADS), np.tile(o_kpe + _swap_halves(1, MLA_ROPE), MLA_HEADS),
        o_qr + np.arange(RET_QK), o_qr + _swap_halves(RET_HEADS, RET_DK),
        o_kr + np.arange(RET_QK), o_kr + _swap_halves(RET_HEADS, RET_DK),
        np.arange(o_v, o_v + 2 * RET_W)])
    assert cols.shape[0] == IN0_AUG
    hd = MLA_NOPE + MLA_ROPE
    heads = np.arange(MLA_HEADS)[:, None] * hd
    uq_cols = np.concatenate([
        (heads + np.arange(MLA_NOPE)[None, :]).ravel(),
        (heads + MLA_NOPE + np.arange(MLA_ROPE)[None, :]).ravel(),
        (heads + MLA_NOPE + _swap_halves(1, MLA_ROPE)[None, :]).ravel()])
    wuk_t = jnp.transpose(w_uk, (1, 2, 0))
    z = jnp.zeros((MLA_NOPE, MLA_KV_RANK), F32)
    wuk_pair = jnp.stack([jnp.block([[wuk_t[2 * p], z], [z, wuk_t[2 * p + 1]]]) for p in range(MLA_HEADS // 2)])
    wuv_pad = jnp.zeros((MLA_HEADS, MLA_KV_RANK, 2 * MLA_V), F32)
    for h in range(MLA_HEADS):
        wuv_pad = wuv_pad.at[h, :, MLA_V * (h % 2):MLA_V * (h % 2 + 1)].set(w_uv[:, h, :])
    return dict(w_in0=w_in[:, cols].astype(BF16), q_norm=q_norm[None, :], w_uq=w_uq[:, uq_cols].astype(BF16),
                kv_norm=kv_norm[None, :], w_uk=wuk_pair.astype(BF16), w_uv=wuv_pad.astype(BF16),
                w_out_a=w_out[:MLA_W].astype(BF16), w_out_b=w_out[MLA_W:].astype(BF16))


def _prep_s5(lam_re, lam_im, log_dt, b_re, b_im, c_re, c_im, d_skip, w_glu):
    lr, li = lam_re.astype(F32), lam_im.astype(F32)
    dt = jnp.exp(log_dt.astype(F32))[:, None]
    mag = jnp.exp(lr * dt)
    ab_re, ab_im = mag * jnp.cos(li * dt), mag * jnp.sin(li * dt)
    nr, ni = ab_re - 1.0, ab_im
    den = lr * lr + li * li
    f_re = (nr * lr + ni * li) / den
    f_im = (ni * lr - nr * li) / den
    bb_re = f_re[..., None] * b_re - f_im[..., None] * b_im
    bb_im = f_re[..., None] * b_im + f_im[..., None] * b_re
    gpb = _S5_CB // S5_GROUP
    eye = jnp.eye(gpb, dtype=F32)
    bd_in = lambda bb: jnp.einsum('mgpc,gh->mgchp', bb.reshape(_S5_NBLK, gpb, S5_STATE, S5_GROUP),
                                  eye).reshape(_S5_NBLK, _S5_CB, _S5_HB)
    bd_out = lambda cc: jnp.einsum('mgcp,gh->mgphc', cc.reshape(_S5_NBLK, gpb, S5_GROUP, S5_STATE),
                                   eye).reshape(_S5_NBLK, _S5_HB, _S5_CB)
    return dict(bbr=bd_in(bb_re).astype(BF16), bbi=bd_in(bb_im).astype(BF16),
                abr=ab_re.reshape(1, S5_HID), abi=ab_im.reshape(1, S5_HID),
                ccr=bd_out(c_re).astype(BF16), ccin=bd_out(-c_im).astype(BF16),
                d=d_skip[None, :], w_glu=w_glu.astype(BF16))


def _tile(n, pref):
    for t in pref:
        if n % t == 0:
            return t
    return n


def _layer_tail(x_tm, layer, gains, ffn, conv_state, *, nb, tm, mixer=None):
    if mixer is not None:
        a, b, w = mixer
        mixer = (a, b, w["w_out_a"], w["w_out_b"], gains[layer, 1][None])
    return _ffn(x_tm, gains[layer, 2][None], ffn["w_up"][layer], ffn["conv_w"][layer], ffn["conv_b"][layer][None],
                ffn["w_down"][layer], gains[layer, 3][None], conv_state, nb=nb, tm=tm, mixer=mixer)


def _run_prompt(x, gains, w0, w1, ws5, ffn):
    nseq, seqlen, _ = x.shape
    rows = nseq * seqlen
    x2d = x.reshape(rows, D_MODEL)
    pos = jnp.arange(seqlen, dtype=jnp.int32)
    tabs = _rope_tables(pos, MLA_ROPE // 2, MLA_HEADS) + _rope_tables(pos, RET_DK // 2, RET_HEADS)
    tm = _tile(seqlen, (512, 256, 128))
    tq = _tile(seqlen, (256, 128))
    qlat, qpe, kfull, rows_new, qr, kr, vr, sg = _inproj0(x2d, gains[0, 0][None], w0, tabs, nseq=nseq,
                                                         seqlen=seqlen, tm=tm, idt=BF16)
    o_mla = _mla_prompt(qlat, qpe, kfull, w0["w_uv"], nseq=nseq, seqlen=seqlen, tq=tq)
    o_ret, s_new = _retention(qr, kr, vr, sg, jnp.zeros((nseq, RET_QK, RET_DV), F32), nseq=nseq, seqlen=seqlen,
                              odt=BF16)
    nt = seqlen // tm
    x1 = _outproj(x2d, o_mla, o_ret, w0["w_out_a"], w0["w_out_b"], gains[0, 1][None], grid=(nseq, nt),
                  x_map=lambda b, i: (b * nt + i, 0), ab_map=lambda b, i: (i, b),
                  y_map=lambda b, i: (b * nt + i, 0), y_shape=(rows, D_MODEL), tm=tm)
    x1 = jnp.swapaxes(x1.reshape(nseq, seqlen, D_MODEL), 0, 1).reshape(rows, D_MODEL)
    tmf = _tile(rows, (512, 256, 128, 64))
    zero_state = jnp.zeros((2 * nseq, 2 * D_FF), F32)
    x2, cs0 = _layer_tail(x1, 0, gains, ffn, zero_state, nb=nseq, tm=tmf)
    u, q, k32, v32, kb, vb = _inproj1(x2, gains[1, 0][None], w1["w_in1"], tm=tmf, idt=BF16)
    tc = _tile(seqlen, (16, 8))
    zero_h = jnp.zeros((nseq, S5_HID), F32)
    o_s5, h_re, h_im = _s5(u.reshape(seqlen, nseq, S5_CH), zero_h, zero_h, ws5, tc=tc, bb=nseq, odt=BF16)
    tm2 = lambda a: a.reshape(seqlen, nseq * SB_W)
    o_sb, k_out, v_out = _sb_prompt(tm2(q), tm2(kb), tm2(vb), tm2(k32), tm2(v32), nseq=nseq, seqlen=seqlen, tq=tq)
    x4, cs1 = _layer_tail(x2, 1, gains, ffn, zero_state, nb=nseq, tm=tmf,
                          mixer=(o_s5.reshape(rows, S5_CH), o_sb.reshape(rows, SB_W), w1))
    y = jnp.swapaxes(x4.reshape(seqlen, nseq, D_MODEL), 0, 1)
    conv = jnp.stack([cs0, cs1]).reshape(2, CONV_W - 1, nseq, 2 * D_FF).transpose(0, 2, 1, 3)
    return (y.reshape(nseq, seqlen, D_MODEL), rows_new.reshape(nseq, seqlen, MLA_ROW),
            s_new.reshape(nseq, RET_HEADS, RET_DK, RET_DV), h_re.reshape(nseq, S5_GROUPS, S5_STATE),
            h_im.reshape(nseq, S5_GROUPS, S5_STATE), k_out.reshape(nseq, seqlen, SB_HEADS, SB_DIM),
            v_out.reshape(nseq, seqlen, SB_HEADS, SB_DIM), conv)


def _run_sample(x, gains, w0, w1, ws5, ffn, cache_mla, state_ret, s5_re, s5_im, cache_k, cache_v, state_conv,
                page_table):
    nseq, lnew, _ = x.shape
    rows = nseq * lnew
    past_len = page_table.shape[1] * PAGE_SIZE
    to_tm = lambda a: jnp.swapaxes(a, 0, 1)
    x_tm = to_tm(x).reshape(rows, D_MODEL)
    pos = past_len + jnp.arange(lnew, dtype=jnp.int32)
    pos_rows = jnp.repeat(pos, nseq)
    tabs = _rope_tables(pos_rows, MLA_ROPE // 2, MLA_HEADS) + _rope_tables(pos_rows, RET_DK // 2, RET_HEADS)
    tm = _tile(rows, (512, 256, 128, 64))
    qlat, qpe, _, rows_new, qr, kr, vr, sg = _inproj0(x_tm, gains[0, 0][None], w0, tabs, nseq=1, seqlen=rows,
                                                     tm=tm, idt=F32)
    ql = qlat.reshape(MLA_HEADS, lnew, nseq, MLA_KV_RANK).transpose(2, 0, 1, 3)
    qp = qpe.reshape(lnew, nseq, MLA_HEADS, MLA_ROPE).transpose(1, 2, 0, 3)
    qs = jnp.concatenate([ql, qp], axis=-1).reshape(nseq, MLA_HEADS * lnew, MLA_ROW)
    rows_seq = to_tm(rows_new.reshape(lnew, nseq, MLA_ROW))
    pad_keys = lambda a: jnp.pad(jnp.swapaxes(a, 1, 2), ((0, 0), (0, 0), (0, PAGE_SIZE - lnew)))
    npage = _tile(page_table.shape[1], (32, 16, 8, 4, 2))
    o_mla = _mla_sample(page_table, qs, pad_keys(rows_seq), w0["w_uv"], jnp.transpose(cache_mla, (0, 2, 1)),
                        npage=npage)
    col = lambda a, w: a.reshape(lnew, nseq * w)
    o_ret, s_new = _retention(col(qr, RET_QK), col(kr, RET_QK), col(vr, RET_W), col(sg, RET_W),
                              state_ret.reshape(nseq, RET_QK, RET_DV), nseq=nseq, seqlen=lnew, odt=F32)
    conv_tm = lambda st: jnp.swapaxes(st, 0, 1).reshape(2 * nseq, 2 * D_FF)
    tmf = max(2 * nseq, _tile(rows, (512, 256, 128)))
    x2, cs0 = _layer_tail(x_tm, 0, gains, ffn, conv_tm(state_conv[0]), nb=nseq, tm=tmf,
                          mixer=(to_tm(o_mla).reshape(rows, MLA_W), o_ret.reshape(rows, RET_W), w0))
    u, q, k32, v32, _, _ = _inproj1(x2, gains[1, 0][None], w1["w_in1"], tm=tm, idt=F32)
    bb = _tile(nseq, (64, 32, 16, 8))
    o_s5, h_re, h_im = _s5(u.reshape(lnew, nseq, S5_CH), s5_re.reshape(nseq, S5_HID), s5_im.reshape(nseq, S5_HID),
                           ws5, tc=lnew, bb=bb, odt=F32)
    seq = lambda a: to_tm(a.reshape(lnew, nseq, SB_W))
    k_seq, v_seq = seq(k32), seq(v32)
    q_hd = seq(q).reshape(nseq, lnew, SB_HEADS, SB_DIM).transpose(0, 2, 1, 3)
    q_bd = jnp.einsum('bhtd,hg->bhtgd', q_hd, jnp.eye(SB_HEADS, dtype=F32)).reshape(nseq, SB_HEADS * lnew, SB_W)
    npage_sb = _tile(page_table.shape[1], (32, 16, 8, 4, 2))
    o_sb = _sb_sample(page_table, q_bd, pad_keys(k_seq), pad_keys(v_seq), jnp.transpose(cache_k, (0, 2, 3, 1)),
                      jnp.transpose(cache_v, (0, 2, 3, 1)), npage=npage_sb)
    o_sb_tm = to_tm(o_sb).reshape(rows, SB_W)
    x4, cs1 = _layer_tail(x2, 1, gains, ffn, conv_tm(state_conv[1]), nb=nseq, tm=tmf,
                          mixer=(o_s5.reshape(rows, S5_CH), o_sb_tm, w1))
    y = to_tm(x4.reshape(lnew, nseq, D_MODEL))
    conv = jnp.stack([cs0, cs1]).reshape(2, CONV_W - 1, nseq, 2 * D_FF).transpose(0, 2, 1, 3)
    return (y, rows_seq, s_new.reshape(nseq, RET_HEADS, RET_DK, RET_DV), h_re.reshape(nseq, S5_GROUPS, S5_STATE),
            h_im.reshape(nseq, S5_GROUPS, S5_STATE), k_seq.reshape(nseq, lnew, SB_HEADS, SB_DIM),
            v_seq.reshape(nseq, lnew, SB_HEADS, SB_DIM), conv)


def kernel(x_prompt, x_sample, cache_mla, state_ret, state_s5_re, state_s5_im, cache_sb_k, cache_sb_v,
           state_ffn_conv, page_table, norm_gains, w_in_l0, mla_q_norm, mla_w_uq, mla_kv_norm, mla_w_uk,
           mla_w_uv, w_out_l0, w_in_l1, s5_lam_re, s5_lam_im, s5_log_dt, s5_b_re, s5_b_im, s5_c_re, s5_c_im,
           s5_d, s5_w_glu, w_out_l1, ffn_w_up, ffn_conv_w, ffn_conv_b, ffn_w_down):
    w0 = _prep_layer0(w_in_l0, mla_q_norm, mla_w_uq, mla_kv_norm, mla_w_uk, mla_w_uv, w_out_l0)
    w1 = dict(w_in1=w_in_l1.astype(BF16), w_out_a=w_out_l1[:S5_CH].astype(BF16),
              w_out_b=w_out_l1[S5_CH:].astype(BF16))
    ws5 = _prep_s5(s5_lam_re, s5_lam_im, s5_log_dt, s5_b_re, s5_b_im, s5_c_re, s5_c_im, s5_d, s5_w_glu)
    ffn = dict(w_up=ffn_w_up.astype(BF16), conv_w=ffn_conv_w, conv_b=ffn_conv_b, w_down=ffn_w_down.astype(BF16))
    gains = norm_gains.astype(F32)
    (y_p, mla_p, ret_p, s5r_p, s5i_p, sbk_p, sbv_p, conv_p) = _run_prompt(x_prompt, gains, w0, w1, ws5, ffn)
    (y_s, mla_s, ret_s, s5r_s, s5i_s, sbk_s, sbv_s, conv_s) = _run_sample(
        x_sample, gains, w0, w1, ws5, ffn, cache_mla, state_ret, state_s5_re, state_s5_im, cache_sb_k, cache_sb_v,
        state_ffn_conv, page_table)
    return (y_p, y_s, mla_p, mla_s, ret_p, ret_s, s5r_p, s5i_p, s5r_s, s5i_s, sbk_p, sbv_p, sbk_s, sbv_s,
            conv_p, conv_s)
```

```python
import functools
import math

import jax
import jax.numpy as jnp
import numpy as np
from jax import lax
from jax.experimental import pallas as pl
from jax.experimental.pallas import tpu as pltpu

F32 = jnp.float32
BF16 = jnp.bfloat16

D_MODEL = 1024
PAGE_SIZE = 128
MLA_HEADS = 8
MLA_Q_RANK = 384
MLA_KV_RANK = 256
MLA_NOPE = 64
MLA_ROPE = 32
MLA_V = 64
MLA_ROW = MLA_KV_RANK + MLA_ROPE
MLA_W = MLA_HEADS * MLA_V
RET_HEADS = 4
RET_DK = 64
RET_DV = 128
RET_W = RET_HEADS * RET_DV
RET_QK = RET_HEADS * RET_DK
RET_CHUNK = 128
S5_GROUP = 16
S5_CH = 640
S5_GROUPS = S5_CH // S5_GROUP
S5_STATE = 64
S5_HID = S5_GROUPS * S5_STATE
SB_HEADS = 6
SB_DIM = 64
SB_W = SB_HEADS * SB_DIM
D_FF = 2816
CONV_W = 3
ROPE_THETA = 10000.0
EPS = 1e-6
NEG = -1e30

VMEM_LIMIT_BYTES = 56 * 1024 * 1024
LANES = 128


def _cparams(sem):
    return pltpu.CompilerParams(dimension_semantics=sem, vmem_limit_bytes=VMEM_LIMIT_BYTES)


def _const_spec(shape):
    nd = len(shape)
    return pl.BlockSpec(shape, lambda *_: (0,) * nd, pipeline_mode=pl.Buffered(1))


def _rms(x):
    return x * lax.rsqrt(jnp.mean(x * x, axis=-1, keepdims=True) + EPS)


def _gelu_tanh(x):
    c = math.sqrt(2.0 / math.pi)
    return x * (0.5 * (1.0 + jnp.tanh(c * (x + 0.044715 * (x * x * x)))))


def _dot(a, b):
    return jnp.dot(a, b, preferred_element_type=F32)


def _dot_nt(a, b):
    return lax.dot_general(a, b, (((1,), (1,)), ((), ())), preferred_element_type=F32)


def _lane_band(x, lo, width):
    lane = lax.broadcasted_iota(jnp.int32, x.shape, x.ndim - 1)
    return jnp.where((lane >= lo) & (lane < lo + width), x, jnp.zeros_like(x))


_O_CQ, _O_CKV, _O_KPE, _O_KPES = 0, 384, 640, 896
_O_QR, _O_QRS, _O_KR, _O_KRS, _O_VR, _O_GR = 1152, 1408, 1664, 1920, 2176, 2688
IN0_AUG = 3200
LOG2E = math.log2(math.e)
_MLA_SCALE = (MLA_NOPE + MLA_ROPE) ** -0.5 * LOG2E


def _inproj0_kernel(x_ref, g_ref, win_ref, qn_ref, wuq_ref, kvn_ref, wuk_ref, cm_ref, sm_ref, cr_ref, sr_ref,
                    qlat_ref, qpe_ref, kfull_ref, rows_ref, qr_ref, kr_ref, vr_ref, sg_ref):
    xn = (_rms(x_ref[...]) * g_ref[...]).astype(BF16)
    z = _dot(xn, win_ref[...])
    cqn = (_rms(z[:, _O_CQ:_O_CQ + MLA_Q_RANK]) * qn_ref[...]).astype(BF16)
    q = _dot(cqn, wuq_ref[...])
    for p in range(MLA_HEADS // 2):
        ql = _dot(q[:, 128 * p:128 * (p + 1)].astype(BF16), wuk_ref[p]) * _MLA_SCALE
        qlat_ref[2 * p] = ql[:, :MLA_KV_RANK].astype(qlat_ref.dtype)
        qlat_ref[2 * p + 1] = ql[:, MLA_KV_RANK:].astype(qlat_ref.dtype)
    cm, sm = cm_ref[...], sm_ref[...]
    qpe = (q[:, 512:768] * cm + q[:, 768:1024] * sm) * _MLA_SCALE
    qpe_ref[...] = qpe.astype(qpe_ref.dtype)
    ckv = _rms(z[:, _O_CKV:_O_CKV + MLA_KV_RANK]) * kvn_ref[...]
    kpe = z[:, _O_KPE:_O_KPE + 256] * cm + z[:, _O_KPES:_O_KPES + 256] * sm
    kfull_ref[:, 0:256] = ckv.astype(kfull_ref.dtype)
    kfull_ref[:, 256:512] = kpe.astype(kfull_ref.dtype)
    rows_ref[:, 0:MLA_KV_RANK] = ckv
    rows_ref[:, MLA_KV_RANK:MLA_ROW] = kpe[:, 0:MLA_ROPE]
    cr, sr = cr_ref[...], sr_ref[...]
    qr_ref[...] = z[:, _O_QR:_O_QR + RET_QK] * cr + z[:, _O_QRS:_O_QRS + RET_QK] * sr
    kr_ref[...] = (z[:, _O_KR:_O_KR + RET_QK] * cr + z[:, _O_KRS:_O_KRS + RET_QK] * sr) * (RET_DK ** -0.5)
    vr_ref[...] = z[:, _O_VR:_O_VR + RET_W].astype(vr_ref.dtype)
    g = z[:, _O_GR:_O_GR + RET_W]
    sg_ref[...] = g * jax.nn.sigmoid(g)


def _inproj0(x2d, gain, w, tabs, *, nseq, seqlen, tm, idt):
    nt = seqlen // tm
    cm, sm, cr, sr = tabs
    row_in = lambda b, i: (b * nt + i, 0)
    tm_out = lambda b, i: (i, b)
    tab = lambda b, i: (i, 0)
    out_shape = (
        jax.ShapeDtypeStruct((MLA_HEADS, seqlen, nseq * 256), idt),
        jax.ShapeDtypeStruct((seqlen, nseq * 256), idt),
        jax.ShapeDtypeStruct((seqlen, nseq * 512), idt),
        jax.ShapeDtypeStruct((nseq * seqlen, MLA_ROW), F32),
        jax.ShapeDtypeStruct((seqlen, nseq * RET_QK), F32),
        jax.ShapeDtypeStruct((seqlen, nseq * RET_QK), F32),
        jax.ShapeDtypeStruct((seqlen, nseq * RET_W), idt),
        jax.ShapeDtypeStruct((seqlen, nseq * RET_W), F32),
    )
    out_specs = (
        pl.BlockSpec((MLA_HEADS, tm, 256), lambda b, i: (0, i, b)),
        pl.BlockSpec((tm, 256), tm_out),
        pl.BlockSpec((tm, 512), tm_out),
        pl.BlockSpec((tm, MLA_ROW), row_in),
        pl.BlockSpec((tm, RET_QK), tm_out),
        pl.BlockSpec((tm, RET_QK), tm_out),
        pl.BlockSpec((tm, RET_W), tm_out),
        pl.BlockSpec((tm, RET_W), tm_out),
    )
    in_specs = [
        pl.BlockSpec((tm, D_MODEL), row_in),
        _const_spec((1, D_MODEL)),
        _const_spec((D_MODEL, IN0_AUG)),
        _const_spec((1, MLA_Q_RANK)),
        _const_spec((MLA_Q_RANK, 1024)),
        _const_spec((1, MLA_KV_RANK)),
        _const_spec((MLA_HEADS // 2, 128, 512)),
        pl.BlockSpec((tm, 256), tab), pl.BlockSpec((tm, 256), tab),
        pl.BlockSpec((tm, 256), tab), pl.BlockSpec((tm, 256), tab),
    ]
    return pl.pallas_call(
        _inproj0_kernel, grid=(nseq, nt), in_specs=in_specs, out_specs=out_specs, out_shape=out_shape,
        compiler_params=_cparams(("parallel", "parallel")), name="inproj0",
    )(x2d, gain, w["w_in0"], w["q_norm"], w["w_uq"], w["kv_norm"], w["w_uk"], cm, sm, cr, sr)


def _mla_value_proj(o_lat, wuv_ref, n):
    pairs = []
    for p in range(MLA_HEADS // 2):
        pairs.append(_dot(o_lat[2 * p * n:(2 * p + 1) * n], wuv_ref[2 * p])
                     + _dot(o_lat[(2 * p + 1) * n:(2 * p + 2) * n], wuv_ref[2 * p + 1]))
    return jnp.concatenate(pairs, axis=1)


def _mla_prompt_kernel(qlat_ref, qpe_ref, k_ref, wuv_ref, o_ref, qs_ref, s_ref, mx_ref, ls_ref, acc_ref, *, tq):
    qi = pl.program_id(1)
    rows = MLA_HEADS * tq
    half = tq // 2
    assert half == LANES
    qpe = qpe_ref[...]
    for h in range(MLA_HEADS):
        qs_ref[h * tq:(h + 1) * tq, 0:256] = qlat_ref[h]
        qs_ref[h * tq:(h + 1) * tq, 256:512] = _lane_band(qpe, MLA_ROPE * h, MLA_ROPE)
    mx_ref[...] = jnp.full((rows, LANES), NEG, F32)
    ls_ref[...] = jnp.zeros((rows, LANES), F32)
    acc_ref[...] = jnp.zeros((rows, MLA_KV_RANK), F32)

    def kblock(j):
        return k_ref[pl.ds(pl.multiple_of(j * tq, tq), tq), :]

    def pass1(j, c):
        s = _dot_nt(qs_ref[...], kblock(j))
        s_ref[j] = s
        mx_ref[...] = jnp.maximum(mx_ref[...], jnp.maximum(s[:, :half], s[:, half:]))
        return c

    lax.fori_loop(0, qi, pass1, 0)
    t_q = lax.broadcasted_iota(jnp.int32, (rows, tq), 0) & (tq - 1)
    t_k = lax.broadcasted_iota(jnp.int32, (rows, tq), 1)
    s = jnp.where(t_k <= t_q, _dot_nt(qs_ref[...], kblock(qi)), NEG)
    s_ref[qi] = s
    mx = jnp.maximum(mx_ref[...], jnp.maximum(s[:, :half], s[:, half:]))
    mx_ref[...] = jnp.broadcast_to(jnp.max(mx, axis=-1, keepdims=True), (rows, LANES))

    def pass2(j, c):
        s = s_ref[j]
        m = mx_ref[...]
        p0 = jnp.exp2(s[:, :half] - m)
        p1 = jnp.exp2(s[:, half:] - m)
        ls_ref[...] += p0 + p1
        p = jnp.concatenate([p0, p1], axis=1).astype(BF16)
        acc_ref[...] += _dot(p, kblock(j)[:, 0:MLA_KV_RANK])
        return c

    lax.fori_loop(0, qi + 1, pass2, 0)
    l = jnp.sum(ls_ref[...], axis=-1, keepdims=True)
    o_lat = (acc_ref[...] / l).astype(BF16)
    o_ref[...] = _mla_value_proj(o_lat, wuv_ref, tq).astype(o_ref.dtype)


def _mla_prompt(qlat, qpe, kfull, wuv, *, nseq, seqlen, tq):
    nq = seqlen // tq
    rows = MLA_HEADS * tq
    assert tq & (tq - 1) == 0
    return pl.pallas_call(
        functools.partial(_mla_prompt_kernel, tq=tq),
        grid=(nseq, nq),
        in_specs=[
            pl.BlockSpec((MLA_HEADS, tq, 256), lambda b, i: (0, i, b)),
            pl.BlockSpec((tq, 256), lambda b, i: (i, b)),
            pl.BlockSpec((seqlen, 512), lambda b, i: (0, b)),
            _const_spec((MLA_HEADS, MLA_KV_RANK, 2 * MLA_V)),
        ],
        out_specs=pl.BlockSpec((tq, MLA_W), lambda b, i: (i, b)),
        out_shape=jax.ShapeDtypeStruct((seqlen, nseq * MLA_W), BF16),
        scratch_shapes=[pltpu.VMEM((rows, 512), BF16), pltpu.VMEM((nq, rows, tq), F32),
                        pltpu.VMEM((rows, LANES), F32), pltpu.VMEM((rows, LANES), F32),
                        pltpu.VMEM((rows, MLA_KV_RANK), F32)],
        compiler_params=_cparams(("parallel", "parallel")), name="mla_prompt",
    )(qlat, qpe, kfull, wuv)


def _mla_sample_kernel(pt_ref, qs_ref, knt_ref, wuv_ref, *refs, npage, nstep, lnew):
    pages = refs[:npage]
    o_ref = refs[npage]
    parts = (refs[npage + 1:npage + 4], refs[npage + 4:npage + 7])
    step = pl.program_id(1)
    rows = MLA_HEADS * lnew

    @pl.when(step == 0)
    def _():
        for m_ref, l_ref, acc_ref in parts:
            m_ref[...] = jnp.full((rows, 1), NEG, F32)
            l_ref[...] = jnp.zeros((rows, 1), F32)
            acc_ref[...] = jnp.zeros((rows, MLA_KV_RANK), F32)

    q = qs_ref[0].astype(BF16)

    def update(part, s, kts):
        m_ref, l_ref, acc_ref = part
        m_prev = m_ref[...]
        m_new = jnp.maximum(m_prev, jnp.max(s, axis=-1, keepdims=True))
        alpha = jnp.exp2(m_prev - m_new)
        p = jnp.exp2(s - m_new)
        l_ref[...] = alpha * l_ref[...] + jnp.sum(p, axis=-1, keepdims=True)
        pv = _dot_nt(p[:, 0:PAGE_SIZE].astype(BF16), kts[0][0:MLA_KV_RANK])
        for n in range(1, len(kts)):
            pv = pv + _dot_nt(p[:, n * PAGE_SIZE:(n + 1) * PAGE_SIZE].astype(BF16), kts[n][0:MLA_KV_RANK])
        acc_ref[...] = alpha * acc_ref[...] + pv
        m_ref[...] = m_new

    kts = [pages[n][0].astype(BF16) for n in range(npage)]
    half = max(npage // 2, 1)
    for part, sub in ((parts[0], kts[:half]), (parts[1], kts[half:])):
        if sub:
            update(part, jnp.concatenate([_dot(q, kt) for kt in sub], axis=1), sub)

    @pl.when(step == nstep - 1)
    def _():
        knt = knt_ref[0].astype(BF16)
        sn = _dot(q, knt)
        t_q = lax.broadcasted_iota(jnp.int32, (rows, PAGE_SIZE), 0) & (lnew - 1)
        t_k = lax.broadcasted_iota(jnp.int32, (rows, PAGE_SIZE), 1)
        update(parts[0], jnp.where(t_k <= t_q, sn, NEG), [knt])
        (m0, l0, a0), (m1, l1, a1) = parts
        m = jnp.maximum(m0[...], m1[...])
        w0, w1 = jnp.exp2(m0[...] - m), jnp.exp2(m1[...] - m)
        l = w0 * l0[...] + w1 * l1[...]
        o_lat = ((w0 * a0[...] + w1 * a1[...]) / l).astype(BF16)
        o_ref[0] = _mla_value_proj(o_lat, wuv_ref, lnew)


def _mla_sample(page_table, qs, knew_t, wuv, cache_t, *, npage):
    nseq, n_pages = page_table.shape
    rows = qs.shape[1]
    lnew = rows // MLA_HEADS
    assert lnew & (lnew - 1) == 0 and lnew <= PAGE_SIZE
    nstep = n_pages // npage
    page_specs = [
        pl.BlockSpec((1, MLA_ROW, PAGE_SIZE), lambda b, s, pt, n=n: (pt[b, s * npage + n], 0, 0))
        for n in range(npage)
    ]
    grid_spec = pltpu.PrefetchScalarGridSpec(
        num_scalar_prefetch=1, grid=(nseq, nstep),
        in_specs=[
            pl.BlockSpec((1, rows, MLA_ROW), lambda b, s, pt: (b, 0, 0)),
            pl.BlockSpec((1, MLA_ROW, PAGE_SIZE), lambda b, s, pt: (b, 0, 0)),
            pl.BlockSpec((MLA_HEADS, MLA_KV_RANK, 2 * MLA_V), lambda b, s, pt: (0, 0, 0)),
        ] + page_specs,
        out_specs=pl.BlockSpec((1, lnew, MLA_W), lambda b, s, pt: (b, 0, 0)),
        scratch_shapes=[pltpu.VMEM((rows, 1), F32), pltpu.VMEM((rows, 1), F32),
                        pltpu.VMEM((rows, MLA_KV_RANK), F32)] * 2,
    )
    return pl.pallas_call(
        functools.partial(_mla_sample_kernel, npage=npage, nstep=nstep, lnew=lnew),
        grid_spec=grid_spec,
        out_shape=jax.ShapeDtypeStruct((nseq, lnew, MLA_W), F32),
        compiler_params=_cparams(("parallel", "arbitrary")), name="mla_sample",
    )(page_table, qs, knew_t, wuv, *([cache_t] * npage))


def _retention_kernel(q_ref, k_ref, v_ref, sg_ref, intra_ref, qdec_ref, kdec_ref, cdec_ref, s0_ref,
                      o_ref, so_ref, s_ref, pad_ref, *, c, nchunk, lrows):
    i = pl.program_id(1)

    @pl.when(i == 0)
    def _():
        s_ref[...] = s0_ref[0]

    row = lax.broadcasted_iota(jnp.int32, (RET_QK, RET_DV), 0)
    for ch in range(nchunk):
        if lrows < c:
            pad_ref[...] = jnp.zeros(pad_ref.shape, F32)
            pad_ref[0:lrows, 0:RET_QK] = q_ref[...]
            pad_ref[0:lrows, RET_QK:2 * RET_QK] = k_ref[...]
            pad_ref[0:lrows, 2 * RET_QK:2 * RET_QK + RET_W] = v_ref[...].astype(F32)
            q = pad_ref[:, 0:RET_QK]
            k = pad_ref[:, RET_QK:2 * RET_QK]
            v_all = pad_ref[:, 2 * RET_QK:2 * RET_QK + RET_W].astype(BF16)
        else:
            rs = slice(ch * c, (ch + 1) * c)
            q, k, v_all = q_ref[rs, :], k_ref[rs, :], v_ref[rs, :].astype(BF16)
        qd = q * qdec_ref[...]
        kdt = (k * kdec_ref[...]).T.astype(BF16)
        kb = k.astype(BF16)
        s_old = s_ref[...]
        sb = s_old.astype(BF16)
        s_new = cdec_ref[...] * s_old
        for h in range(RET_HEADS):
            qh = _lane_band(q, RET_DK * h, RET_DK).astype(BF16)
            qdh = _lane_band(qd, RET_DK * h, RET_DK).astype(BF16)
            vh = v_all[:, RET_DV * h:RET_DV * (h + 1)]
            att = _dot_nt(qh, kb) * intra_ref[h]
            o = _dot(att.astype(BF16), vh) + _dot(qdh, sb)
            o = _rms(o)
            if lrows < c:
                o_ref[:, RET_DV * h:RET_DV * (h + 1)] = (
                    o[0:lrows] * sg_ref[:, RET_DV * h:RET_DV * (h + 1)]).astype(o_ref.dtype)
            else:
                o_ref[rs, RET_DV * h:RET_DV * (h + 1)] = (
                    o * sg_ref[rs, RET_DV * h:RET_DV * (h + 1)]).astype(o_ref.dtype)
            upd = _dot(kdt, vh)
            s_new = s_new + jnp.where((row >= RET_DK * h) & (row < RET_DK * (h + 1)), upd, 0.0)
        s_ref[...] = s_new

    @pl.when(i == pl.num_programs(1) - 1)
    def _():
        so_ref[0] = s_ref[...]


def _retention_tables(c_eff, c_pad):
    log_g = jnp.log(1.0 - 2.0 ** (-5.0 - jnp.arange(RET_HEADS, dtype=F32)))
    i = jnp.arange(c_pad, dtype=F32)
    diff = i[:, None] - i[None, :]
    intra = jnp.where(diff >= 0, jnp.exp(jnp.maximum(diff, 0.0)[None] * log_g[:, None, None]), 0.0)
    q_dec = jnp.exp((i[:, None] + 1.0) * log_g[None, :])
    k_dec = jnp.exp(jnp.maximum(c_eff - 1.0 - i, 0.0)[:, None] * log_g[None, :])
    c_dec = jnp.exp(c_eff * log_g)
    rep = lambda a: jnp.repeat(a, RET_DK, axis=1)
    cdec_rows = jnp.broadcast_to(jnp.repeat(c_dec, RET_DK)[:, None], (RET_QK, RET_DV))
    return intra.astype(F32), rep(q_dec), rep(k_dec), cdec_rows


def _retention(qr, kr, vr, sg, s0, *, nseq, seqlen, odt):
    c = RET_CHUNK
    if seqlen % c == 0:
        c_eff, tr = c, _tile(seqlen, (8 * c, 4 * c, 2 * c, c))
        nchunk, lrows = tr // c, tr
    else:
        c_eff, tr, nchunk, lrows = seqlen, seqlen, 1, seqlen
    nt = seqlen // tr
    intra, qdec, kdec, cdec = _retention_tables(c_eff, c)
    blk = lambda w: pl.BlockSpec((tr, w), lambda b, i: (i, b))
    return pl.pallas_call(
        functools.partial(_retention_kernel, c=c, nchunk=nchunk, lrows=lrows),
        grid=(nseq, nt),
        in_specs=[blk(RET_QK), blk(RET_QK), blk(RET_W), blk(RET_W),
                  _const_spec((RET_HEADS, c, c)), _const_spec((c, RET_QK)), _const_spec((c, RET_QK)),
                  _const_spec((RET_QK, RET_DV)),
                  pl.BlockSpec((1, RET_QK, RET_DV), lambda b, i: (b, 0, 0))],
        out_specs=(blk(RET_W), pl.BlockSpec((1, RET_QK, RET_DV), lambda b, i: (b, 0, 0))),
        out_shape=(jax.ShapeDtypeStruct((seqlen, nseq * RET_W), odt),
                   jax.ShapeDtypeStruct((nseq, RET_QK, RET_DV), F32)),
        scratch_shapes=[pltpu.VMEM((RET_QK, RET_DV), F32), pltpu.VMEM((c, 2 * RET_QK + RET_W), F32)],
        compiler_params=_cparams(("parallel", "arbitrary")), name="retention",
    )(qr, kr, vr, sg, intra, qdec, kdec, cdec, s0)


def _outproj_kernel(x_ref, a_ref, b_ref, wa_ref, wb_ref, g_ref, y_ref):
    m = _dot(a_ref[...].astype(BF16), wa_ref[...]) + _dot(b_ref[...].astype(BF16), wb_ref[...])
    y_ref[...] = x_ref[...] + _rms(m) * g_ref[...]


def _outproj(x, a, b, wa, wb, gain, *, grid, x_map, ab_map, y_map, y_shape, tm):
    wa_w, wb_w = wa.shape[0], wb.shape[0]
    return pl.pallas_call(
        _outproj_kernel, grid=grid,
        in_specs=[pl.BlockSpec((tm, D_MODEL), x_map), pl.BlockSpec((tm, wa_w), ab_map),
                  pl.BlockSpec((tm, wb_w), ab_map), _const_spec(wa.shape), _const_spec(wb.shape),
                  _const_spec((1, D_MODEL))],
        out_specs=pl.BlockSpec((tm, D_MODEL), y_map),
        out_shape=jax.ShapeDtypeStruct(y_shape, F32),
        compiler_params=_cparams(("parallel",) * len(grid)), name="outproj",
    )(x, a, b, wa, wb, gain)


_FFN_CK = 256


def _ffn_kernel(*refs, nb, tm, mixer):
    if mixer:
        x_ref, a_ref, b_ref, wa_ref, wb_ref, g1_ref = refs[:6]
        refs = refs[6:]
    else:
        x_ref = refs[0]
        refs = refs[1:]
    g2_ref, wup_ref, cw_ref, cb_ref, wdn_ref, g3_ref, st_ref, y_ref, cs_ref, carry_ref = refs
    i = pl.program_id(0)
    ck = _FFN_CK

    @pl.when(i == 0)
    def _():
        carry_ref[...] = st_ref[...]

    x = x_ref[...]
    if mixer:
        m = _dot(a_ref[...].astype(BF16), wa_ref[...]) + _dot(b_ref[...].astype(BF16), wb_ref[...])
        x = x + _rms(m) * g1_ref[...]
    xn = (_rms(x) * g2_ref[...]).astype(BF16)
    acc = jnp.zeros((tm, D_MODEL), F32)
    for c in range(D_FF // ck):
        act = []
        for half in range(2):
            col = half * D_FF + c * ck
            cs = slice(col, col + ck)
            h = _dot(xn, wup_ref[:, cs])
            prev2, prev1 = carry_ref[0:nb, cs], carry_ref[nb:2 * nb, cs]
            carry_ref[:, cs] = h[tm - 2 * nb:tm]
            hm2 = jnp.concatenate([prev2, prev1, h[0:tm - 2 * nb]], axis=0)
            hm1 = jnp.concatenate([prev1, h[0:tm - nb]], axis=0)
            act.append(cb_ref[:, cs] + cw_ref[0:1, cs] * hm2 + cw_ref[1:2, cs] * hm1 + cw_ref[2:3, cs] * h)
        a = (_gelu_tanh(act[0]) * act[1]).astype(BF16)
        acc = acc + _dot(a, wdn_ref[c * ck:(c + 1) * ck, :])
    y_ref[...] = x + _rms(acc) * g3_ref[...]

    @pl.when(i == pl.num_programs(0) - 1)
    def _():
        cs_ref[...] = carry_ref[...]


def _ffn(x, g2, wup, cw, cb, wdn, g3, state, *, nb, tm, mixer=None):
    rows = x.shape[0]
    assert tm >= 2 * nb and tm % nb == 0 and rows % tm == 0
    blk = lambda w: pl.BlockSpec((tm, w), lambda i: (i, 0))
    ins, specs = [x], [blk(D_MODEL)]
    if mixer is not None:
        a, b, wa, wb, g1 = mixer
        ins += [a, b, wa, wb, g1]
        specs += [blk(a.shape[1]), blk(b.shape[1]), _const_spec(wa.shape), _const_spec(wb.shape),
                  _const_spec((1, D_MODEL))]
    ins += [g2, wup, cw, cb, wdn, g3, state]
    specs += [_const_spec((1, D_MODEL)), _const_spec((D_MODEL, 2 * D_FF)), _const_spec((CONV_W, 2 * D_FF)),
              _const_spec((1, 2 * D_FF)), _const_spec((D_FF, D_MODEL)), _const_spec((1, D_MODEL)),
              _const_spec((2 * nb, 2 * D_FF))]
    return pl.pallas_call(
        functools.partial(_ffn_kernel, nb=nb, tm=tm, mixer=mixer is not None),
        grid=(rows // tm,),
        in_specs=specs,
        out_specs=(blk(D_MODEL), pl.BlockSpec((2 * nb, 2 * D_FF), lambda i: (0, 0))),
        out_shape=(jax.ShapeDtypeStruct((rows, D_MODEL), F32), jax.ShapeDtypeStruct((2 * nb, 2 * D_FF), F32)),
        scratch_shapes=[pltpu.VMEM((2 * nb, 2 * D_FF), F32)],
        compiler_params=_cparams(("arbitrary",)), name="convffn",
    )(*ins)


def _inproj1_kernel(x_ref, g_ref, win_ref, u_ref, q_ref, k_ref, v_ref, kb_ref, vb_ref):
    xn = (_rms(x_ref[...]) * g_ref[...]).astype(BF16)
    z = _dot(xn, win_ref[...])
    u_ref[...] = z[:, 0:S5_CH]
    q_ref[...] = (z[:, S5_CH:S5_CH + SB_W] * (SB_DIM ** -0.5)).astype(q_ref.dtype)
    k = z[:, S5_CH + SB_W:S5_CH + 2 * SB_W]
    v = z[:, S5_CH + 2 * SB_W:S5_CH + 3 * SB_W]
    k_ref[...] = k
    v_ref[...] = v
    kb_ref[...] = k.astype(kb_ref.dtype)
    vb_ref[...] = v.astype(vb_ref.dtype)


def _inproj1(x, gain, win, *, tm, idt):
    rows = x.shape[0]
    blk = lambda w: pl.BlockSpec((tm, w), lambda i: (i, 0))
    sds = lambda w, dt: jax.ShapeDtypeStruct((rows, w), dt)
    return pl.pallas_call(
        _inproj1_kernel, grid=(rows // tm,),
        in_specs=[blk(D_MODEL), _const_spec((1, D_MODEL)), _const_spec(win.shape)],
        out_specs=(blk(S5_CH), blk(SB_W), blk(SB_W), blk(SB_W), blk(SB_W), blk(SB_W)),
        out_shape=(sds(S5_CH, F32), sds(SB_W, idt), sds(SB_W, F32), sds(SB_W, F32), sds(SB_W, idt), sds(SB_W, idt)),
        compiler_params=_cparams(("parallel",)), name="inproj1",
    )(x, gain, win)


_S5_LW = 256
_S5_CB = 128
_S5_NBLK = S5_CH // _S5_CB
_S5_HB = S5_HID // _S5_NBLK


def _s5_kernel(u_ref, h0r_ref, h0i_ref, br_ref, bi_ref, ar_ref, ai_ref, cr_ref, cin_ref, d_ref, wg_ref,
               o_ref, hr_ref, hi_ref, *, tc, bb):
    i = pl.program_id(1)

    @pl.when(i == 0)
    def _():
        hr_ref[...] = h0r_ref[...]
        hi_ref[...] = h0i_ref[...]

    u = u_ref[...].reshape(tc * bb, S5_CH)
    ub = u.astype(BF16)
    ys = []
    for m in range(_S5_NBLK):
        um = ub[:, _S5_CB * m:_S5_CB * (m + 1)]
        sr, si = _dot(um, br_ref[m]), _dot(um, bi_ref[m])
        hre, him = [], []
        for sub in range(_S5_HB // _S5_LW):
            ls = slice(sub * _S5_LW, (sub + 1) * _S5_LW)
            cs = slice(_S5_HB * m + sub * _S5_LW, _S5_HB * m + (sub + 1) * _S5_LW)
            ar, ai = ar_ref[:, cs], ai_ref[:, cs]
            hr, hi = hr_ref[:, cs], hi_ref[:, cs]
            hrs, his = [], []
            for t in range(tc):
                rs = slice(t * bb, (t + 1) * bb)
                hr, hi = ar * hr - ai * hi + sr[rs, ls], ar * hi + ai * hr + si[rs, ls]
                hrs.append(hr)
                his.append(hi)
            hr_ref[:, cs] = hr
            hi_ref[:, cs] = hi
            hre.append(jnp.concatenate(hrs, axis=0))
            him.append(jnp.concatenate(his, axis=0))
        ys.append(_dot(jnp.concatenate(hre, axis=1).astype(BF16), cr_ref[m])
                  + _dot(jnp.concatenate(him, axis=1).astype(BF16), cin_ref[m]))
    y = jnp.concatenate(ys, axis=1)
    y = _gelu_tanh(y + d_ref[...] * u)
    o = y * jax.nn.sigmoid(_dot(y.astype(BF16), wg_ref[...]))
    o_ref[...] = o.reshape(tc, bb, S5_CH).astype(o_ref.dtype)


def _s5(u3, h0r, h0i, w, *, tc, bb, odt):
    tlen, nseq, _ = u3.shape
    ublk = pl.BlockSpec((tc, bb, S5_CH), lambda j, i: (i, j, 0))
    hblk = pl.BlockSpec((bb, S5_HID), lambda j, i: (j, 0))
    return pl.pallas_call(
        functools.partial(_s5_kernel, tc=tc, bb=bb),
        grid=(nseq // bb, tlen // tc),
        in_specs=[ublk, hblk, hblk, _const_spec((_S5_NBLK, _S5_CB, _S5_HB)), _const_spec((_S5_NBLK, _S5_CB, _S5_HB)),
                  _const_spec((1, S5_HID)), _const_spec((1, S5_HID)), _const_spec((_S5_NBLK, _S5_HB, _S5_CB)),
                  _const_spec((_S5_NBLK, _S5_HB, _S5_CB)), _const_spec((1, S5_CH)), _const_spec((S5_CH, S5_CH))],
        out_specs=(ublk, hblk, hblk),
        out_shape=(jax.ShapeDtypeStruct((tlen, nseq, S5_CH), odt), jax.ShapeDtypeStruct((nseq, S5_HID), F32),
                   jax.ShapeDtypeStruct((nseq, S5_HID), F32)),
        compiler_params=_cparams(("parallel", "arbitrary")), name="s5",
    )(u3, h0r, h0i, w["bbr"], w["bbi"], w["abr"], w["abi"], w["ccr"], w["ccin"], w["d"], w["w_glu"])


def _sb_weights(z, tri, carry, valid):
    lsz, lk, cum = _sb_logs(z, tri, valid)
    w = jnp.exp(lsz + (cum + carry))
    if valid is not None:
        w = jnp.where(valid, w, 0.0)
    return w.astype(BF16), jnp.sum(lk, axis=-1, keepdims=True)


def _sb_logs(z, tri, valid):
    t = jnp.log(1.0 + jnp.exp2(jnp.abs(z) * (-LOG2E)))
    lsz = jnp.minimum(z, 0.0) - t
    lk = lsz - z
    if valid is not None:
        lk = jnp.where(valid, lk, 0.0)
    hi = lk.astype(BF16)
    lo = (lk - hi.astype(F32)).astype(BF16)
    return lsz, lk, _dot(jnp.concatenate([hi, lo], axis=1), tri)


def _sb_prompt_kernel(q_ref, kb_ref, vb_ref, k32_ref, v32_ref, tri_ref, o_ref, ko_ref, vo_ref,
                      qm_ref, carry_ref, acc_ref, *, tq):
    qi = pl.program_id(1)
    rows = SB_HEADS * tq
    ko_ref[...] = k32_ref[...]
    vo_ref[...] = v32_ref[...]
    tri = tri_ref[...]
    for h in range(SB_HEADS):
        qm_ref[h] = _lane_band(q_ref[:, LANES * (h // 2):LANES * (h // 2 + 1)], SB_DIM * (h % 2), SB_DIM)
    carry_ref[...] = jnp.zeros(carry_ref.shape, F32)
    acc_ref[...] = jnp.zeros(acc_ref.shape, F32)

    def visit(j, valid):
        rs = pl.ds(pl.multiple_of(j * tq, tq), tq)
        ps = [slice(LANES * (h // 2), LANES * (h // 2 + 1)) for h in range(SB_HEADS)]
        z = jnp.concatenate([_dot_nt(qm_ref[h], kb_ref[rs, ps[h]]) for h in range(SB_HEADS)], axis=0)
        w, lksum = _sb_weights(z, tri, carry_ref[...], valid)
        for h in range(SB_HEADS):
            acc_ref[h] += _dot(w[h * tq:(h + 1) * tq], vb_ref[rs, ps[h]])
        carry_ref[...] += lksum

    t_q = lax.broadcasted_iota(jnp.int32, (rows, tq), 0) & (tq - 1)
    t_k = lax.broadcasted_iota(jnp.int32, (rows, tq), 1)
    visit(qi, t_k < t_q)

    def body(jj, c):
        visit(qi - 1 - jj, None)
        return c

    lax.fori_loop(0, qi, body, 0)
    lane = lax.broadcasted_iota(jnp.int32, (tq, LANES), 1)
    for p in range(SB_HEADS // 2):
        o_ref[:, LANES * p:LANES * (p + 1)] = jnp.where(
            lane < SB_DIM, acc_ref[2 * p], acc_ref[2 * p + 1]).astype(o_ref.dtype)


def _tri(n):
    j = jnp.arange(n)
    t = (j[:, None] > j[None, :]).astype(BF16)
    return jnp.concatenate([t, t], axis=0)


def _sb_prompt(q, kb, vb, k32, v32, *, nseq, seqlen, tq):
    nq = seqlen // tq
    colblk = pl.BlockSpec((tq, SB_W), lambda b, i: (i, b))
    seqblk = pl.BlockSpec((seqlen, SB_W), lambda b, i: (0, b))
    rowblk = pl.BlockSpec((tq, SB_W), lambda b, i: (b * nq + i, 0))
    return pl.pallas_call(
        functools.partial(_sb_prompt_kernel, tq=tq),
        grid=(nseq, nq),
        in_specs=[colblk, seqblk, seqblk, colblk, colblk, _const_spec((2 * tq, tq))],
        out_specs=(colblk, rowblk, rowblk),
        out_shape=(jax.ShapeDtypeStruct((seqlen, nseq * SB_W), BF16),
                   jax.ShapeDtypeStruct((nseq * seqlen, SB_W), F32),
                   jax.ShapeDtypeStruct((nseq * seqlen, SB_W), F32)),
        scratch_shapes=[pltpu.VMEM((SB_HEADS, tq, LANES), BF16), pltpu.VMEM((SB_HEADS * tq, 1), F32),
                        pltpu.VMEM((SB_HEADS, tq, LANES), F32)],
        compiler_params=_cparams(("parallel", "parallel")), name="sb_prompt",
    )(q, kb, vb, k32, v32, _tri(tq))


def _sb_sample_kernel(pt_ref, q_ref, knt_ref, vnt_ref, tri_ref, *refs, npage, nstep, lnew):
    kpages = refs[:npage]
    vpages = refs[npage:2 * npage]
    o_ref = refs[2 * npage]
    carry_ref, acc_ref = refs[2 * npage + 1:]
    step = pl.program_id(1)
    rows = SB_HEADS * lnew
    tri = tri_ref[...]
    q = q_ref[0].astype(BF16)

    @pl.when(step == 0)
    def _():
        t_q = lax.broadcasted_iota(jnp.int32, (rows, PAGE_SIZE), 0) & (lnew - 1)
        t_k = lax.broadcasted_iota(jnp.int32, (rows, PAGE_SIZE), 1)
        w, lksum = _sb_weights(_dot(q, knt_ref[0].astype(BF16)), tri, jnp.zeros((rows, 1), F32), t_k < t_q)
        acc_ref[...] = _dot_nt(w, vnt_ref[0].astype(BF16))
        carry_ref[...] = lksum

    z = jnp.concatenate([_dot(q, kpages[n][0].reshape(SB_W, PAGE_SIZE).astype(BF16)) for n in range(npage)],
                        axis=0)
    lsz, lk, cum = _sb_logs(z, tri, None)
    lksum = jnp.sum(lk, axis=-1, keepdims=True)
    carries, carry = [], carry_ref[...]
    for n in range(npage):
        carries.append(carry)
        carry = carry + lksum[n * rows:(n + 1) * rows]
    carry_ref[...] = carry
    w = jnp.exp(lsz + cum + jnp.concatenate(carries, axis=0)).astype(BF16)
    pv = _dot_nt(w[0:rows], vpages[0][0].reshape(SB_W, PAGE_SIZE).astype(BF16))
    for n in range(1, npage):
        pv = pv + _dot_nt(w[n * rows:(n + 1) * rows], vpages[n][0].reshape(SB_W, PAGE_SIZE).astype(BF16))
    acc_ref[...] += pv

    @pl.when(step == nstep - 1)
    def _():
        acc = acc_ref[...]
        out = _lane_band(acc[0:lnew], 0, SB_DIM)
        for h in range(1, SB_HEADS):
            out = out + _lane_band(acc[h * lnew:(h + 1) * lnew], SB_DIM * h, SB_DIM)
        o_ref[0] = out


def _sb_sample(page_table, q_bd, knew_t, vnew_t, cache_kt, cache_vt, *, npage):
    nseq, n_pages = page_table.shape
    rows = q_bd.shape[1]
    lnew = rows // SB_HEADS
    assert lnew & (lnew - 1) == 0 and lnew <= PAGE_SIZE
    nstep = n_pages // npage
    pidx = lambda b, s, pt, n: (pt[b, n_pages - 1 - (s * npage + n)], 0, 0, 0)
    pspecs = [pl.BlockSpec((1, SB_HEADS, SB_DIM, PAGE_SIZE), functools.partial(pidx, n=n)) for n in range(npage)]
    grid_spec = pltpu.PrefetchScalarGridSpec(
        num_scalar_prefetch=1, grid=(nseq, nstep),
        in_specs=[
            pl.BlockSpec((1, rows, SB_W), lambda b, s, pt: (b, 0, 0)),
            pl.BlockSpec((1, SB_W, PAGE_SIZE), lambda b, s, pt: (b, 0, 0)),
            pl.BlockSpec((1, SB_W, PAGE_SIZE), lambda b, s, pt: (b, 0, 0)),
            pl.BlockSpec((2 * PAGE_SIZE, PAGE_SIZE), lambda b, s, pt: (0, 0)),
        ] + pspecs + pspecs,
        out_specs=pl.BlockSpec((1, lnew, SB_W), lambda b, s, pt: (b, 0, 0)),
        scratch_shapes=[pltpu.VMEM((rows, 1), F32), pltpu.VMEM((rows, SB_W), F32)],
    )
    return pl.pallas_call(
        functools.partial(_sb_sample_kernel, npage=npage, nstep=nstep, lnew=lnew),
        grid_spec=grid_spec,
        out_shape=jax.ShapeDtypeStruct((nseq, lnew, SB_W), F32),
        compiler_params=_cparams(("parallel", "arbitrary")), name="sb_sample",
    )(page_table, q_bd, knew_t, vnew_t, _tri(PAGE_SIZE), *([cache_kt] * npage), *([cache_vt] * npage))


def _swap_halves(n_heads, dim):
    idx = np.arange(n_heads * dim)
    return (idx // dim) * dim + (idx % dim + dim // 2) % dim


def _rope_tables(pos, half, reps):
    freqs = ROPE_THETA ** (-jnp.arange(half, dtype=F32) / half)
    ang = pos.astype(F32)[:, None] * freqs[None, :]
    c, s = jnp.cos(ang), jnp.sin(ang)
    return (jnp.tile(jnp.concatenate([c, c], axis=-1), (1, reps)),
            jnp.tile(jnp.concatenate([-s, s], axis=-1), (1, reps)))


def _prep_layer0(w_in, q_norm, w_uq, kv_norm, w_uk, w_uv, w_out):
    o_kpe = MLA_Q_RANK + MLA_KV_RANK
    o_qr = o_kpe + MLA_ROPE
    o_kr = o_qr + RET_QK
    o_v = o_kr + RET_QK
    cols = np.concatenate([
        np.arange(0, o_kpe),
        np.tile(o_kpe + np.arange(MLA_ROPE), MLA_HEADS), np.tile(o_kpe + _swap_halves(1, MLA_ROPE), MLA_HEADS),
        o_qr + np.arange(RET_QK), o_qr + _swap_halves(RET_HEADS, RET_DK),
        o_kr + np.arange(RET_QK), o_kr + _swap_halves(RET_HEADS, RET_DK),
        np.arange(o_v, o_v + 2 * RET_W)])
    assert cols.shape[0] == IN0_AUG
    hd = MLA_NOPE + MLA_ROPE
    heads = np.arange(MLA_HEADS)[:, None] * hd
    uq_cols = np.concatenate([
        (heads + np.arange(MLA_NOPE)[None, :]).ravel(),
        (heads + MLA_NOPE + np.arange(MLA_ROPE)[None, :]).ravel(),
        (heads + MLA_NOPE + _swap_halves(1, MLA_ROPE)[None, :]).ravel()])
    wuk_t = jnp.transpose(w_uk, (1, 2, 0))
    z = jnp.zeros((MLA_NOPE, MLA_KV_RANK), F32)
    wuk_pair = jnp.stack([jnp.block([[wuk_t[2 * p], z], [z, wuk_t[2 * p + 1]]]) for p in range(MLA_HEADS // 2)])
    wuv_pad = jnp.zeros((MLA_HEADS, MLA_KV_RANK, 2 * MLA_V), F32)
    for h in range(MLA_HEADS):
        wuv_pad = wuv_pad.at[h, :, MLA_V * (h % 2):MLA_V * (h % 2 + 1)].set(w_uv[:, h, :])
    return dict(w_in0=w_in[:, cols].astype(BF16), q_norm=q_norm[None, :], w_uq=w_uq[:, uq_cols].astype(BF16),
                kv_norm=kv_norm[None, :], w_uk=wuk_pair.astype(BF16), w_uv=wuv_pad.astype(BF16),
                w_out_a=w_out[:MLA_W].astype(BF16), w_out_b=w_out[MLA_W:].astype(BF16))


def _prep_s5(lam_re, lam_im, log_dt, b_re, b_im, c_re, c_im, d_skip, w_glu):
    lr, li = lam_re.astype(F32), lam_im.astype(F32)
    dt = jnp.exp(log_dt.astype(F32))[:, None]
    mag = jnp.exp(lr * dt)
    ab_re, ab_im = mag * jnp.cos(li * dt), mag * jnp.sin(li * dt)
    nr, ni = ab_re - 1.0, ab_im
    den = lr * lr + li * li
    f_re = (nr * lr + ni * li) / den
    f_im = (ni * lr - nr * li) / den
    bb_re = f_re[..., None] * b_re - f_im[..., None] * b_im
    bb_im = f_re[..., None] * b_im + f_im[..., None] * b_re
    gpb = _S5_CB // S5_GROUP
    eye = jnp.eye(gpb, dtype=F32)
    bd_in = lambda bb: jnp.einsum('mgpc,gh->mgchp', bb.reshape(_S5_NBLK, gpb, S5_STATE, S5_GROUP),
                                  eye).reshape(_S5_NBLK, _S5_CB, _S5_HB)
    bd_out = lambda cc: jnp.einsum('mgcp,gh->mgphc', cc.reshape(_S5_NBLK, gpb, S5_GROUP, S5_STATE),
                                   eye).reshape(_S5_NBLK, _S5_HB, _S5_CB)
    return dict(bbr=bd_in(bb_re).astype(BF16), bbi=bd_in(bb_im).astype(BF16),
                abr=ab_re.reshape(1, S5_HID), abi=ab_im.reshape(1, S5_HID),
                ccr=bd_out(c_re).astype(BF16), ccin=bd_out(-c_im).astype(BF16),
                d=d_skip[None, :], w_glu=w_glu.astype(BF16))


def _tile(n, pref):
    for t in pref:
        if n % t == 0:
            return t
    return n


def _layer_tail(x_tm, layer, gains, ffn, conv_state, *, nb, tm, mixer=None):
    if mixer is not None:
        a, b, w = mixer
        mixer = (a, b, w["w_out_a"], w["w_out_b"], gains[layer, 1][None])
    return _ffn(x_tm, gains[layer, 2][None], ffn["w_up"][layer], ffn["conv_w"][layer], ffn["conv_b"][layer][None],
                ffn["w_down"][layer], gains[layer, 3][None], conv_state, nb=nb, tm=tm, mixer=mixer)


def _run_prompt(x, gains, w0, w1, ws5, ffn):
    nseq, seqlen, _ = x.shape
    rows = nseq * seqlen
    x2d = x.reshape(rows, D_MODEL)
    pos = jnp.arange(seqlen, dtype=jnp.int32)
    tabs = _rope_tables(pos, MLA_ROPE // 2, MLA_HEADS) + _rope_tables(pos, RET_DK // 2, RET_HEADS)
    tm = _tile(seqlen, (512, 256, 128))
    tq = _tile(seqlen, (256, 128))
    qlat, qpe, kfull, rows_new, qr, kr, vr, sg = _inproj0(x2d, gains[0, 0][None], w0, tabs, nseq=nseq,
                                                         seqlen=seqlen, tm=tm, idt=BF16)
    o_mla = _mla_prompt(qlat, qpe, kfull, w0["w_uv"], nseq=nseq, seqlen=seqlen, tq=tq)
    o_ret, s_new = _retention(qr, kr, vr, sg, jnp.zeros((nseq, RET_QK, RET_DV), F32), nseq=nseq, seqlen=seqlen,
                              odt=BF16)
    nt = seqlen // tm
    x1 = _outproj(x2d, o_mla, o_ret, w0["w_out_a"], w0["w_out_b"], gains[0, 1][None], grid=(nseq, nt),
                  x_map=lambda b, i: (b * nt + i, 0), ab_map=lambda b, i: (i, b),
                  y_map=lambda b, i: (b * nt + i, 0), y_shape=(rows, D_MODEL), tm=tm)
    x1 = jnp.swapaxes(x1.reshape(nseq, seqlen, D_MODEL), 0, 1).reshape(rows, D_MODEL)
    tmf = _tile(rows, (512, 256, 128, 64))
    zero_state = jnp.zeros((2 * nseq, 2 * D_FF), F32)
    x2, cs0 = _layer_tail(x1, 0, gains, ffn, zero_state, nb=nseq, tm=tmf)
    u, q, k32, v32, kb, vb = _inproj1(x2, gains[1, 0][None], w1["w_in1"], tm=tmf, idt=BF16)
    tc = _tile(seqlen, (16, 8))
    zero_h = jnp.zeros((nseq, S5_HID), F32)
    o_s5, h_re, h_im = _s5(u.reshape(seqlen, nseq, S5_CH), zero_h, zero_h, ws5, tc=tc, bb=nseq, odt=BF16)
    tm2 = lambda a: a.reshape(seqlen, nseq * SB_W)
    o_sb, k_out, v_out = _sb_prompt(tm2(q), tm2(kb), tm2(vb), tm2(k32), tm2(v32), nseq=nseq, seqlen=seqlen, tq=tq)
    x4, cs1 = _layer_tail(x2, 1, gains, ffn, zero_state, nb=nseq, tm=tmf,
                          mixer=(o_s5.reshape(rows, S5_CH), o_sb.reshape(rows, SB_W), w1))
    y = jnp.swapaxes(x4.reshape(seqlen, nseq, D_MODEL), 0, 1)
    conv = jnp.stack([cs0, cs1]).reshape(2, CONV_W - 1, nseq, 2 * D_FF).transpose(0, 2, 1, 3)
    return (y.reshape(nseq, seqlen, D_MODEL), rows_new.reshape(nseq, seqlen, MLA_ROW),
            s_new.reshape(nseq, RET_HEADS, RET_DK, RET_DV), h_re.reshape(nseq, S5_GROUPS, S5_STATE),
            h_im.reshape(nseq, S5_GROUPS, S5_STATE), k_out.reshape(nseq, seqlen, SB_HEADS, SB_DIM),
            v_out.reshape(nseq, seqlen, SB_HEADS, SB_DIM), conv)


def _run_sample(x, gains, w0, w1, ws5, ffn, cache_mla, state_ret, s5_re, s5_im, cache_k, cache_v, state_conv,
                page_table):
    nseq, lnew, _ = x.shape
    rows = nseq * lnew
    past_len = page_table.shape[1] * PAGE_SIZE
    to_tm = lambda a: jnp.swapaxes(a, 0, 1)
    x_tm = to_tm(x).reshape(rows, D_MODEL)
    pos = past_len + jnp.arange(lnew, dtype=jnp.int32)
    pos_rows = jnp.repeat(pos, nseq)
    tabs = _rope_tables(pos_rows, MLA_ROPE // 2, MLA_HEADS) + _rope_tables(pos_rows, RET_DK // 2, RET_HEADS)
    tm = _tile(rows, (512, 256, 128, 64))
    qlat, qpe, _, rows_new, qr, kr, vr, sg = _inproj0(x_tm, gains[0, 0][None], w0, tabs, nseq=1, seqlen=rows,
                                                     tm=tm, idt=F32)
    ql = qlat.reshape(MLA_HEADS, lnew, nseq, MLA_KV_RANK).transpose(2, 0, 1, 3)
    qp = qpe.reshape(lnew, nseq, MLA_HEADS, MLA_ROPE).transpose(1, 2, 0, 3)
    qs = jnp.concatenate([ql, qp], axis=-1).reshape(nseq, MLA_HEADS * lnew, MLA_ROW)
    rows_seq = to_tm(rows_new.reshape(lnew, nseq, MLA_ROW))
    pad_keys = lambda a: jnp.pad(jnp.swapaxes(a, 1, 2), ((0, 0), (0, 0), (0, PAGE_SIZE - lnew)))
    npage = _tile(page_table.shape[1], (32, 16, 8, 4, 2))
    o_mla = _mla_sample(page_table, qs, pad_keys(rows_seq), w0["w_uv"], jnp.transpose(cache_mla, (0, 2, 1)),
                        npage=npage)
    col = lambda a, w: a.reshape(lnew, nseq * w)
    o_ret, s_new = _retention(col(qr, RET_QK), col(kr, RET_QK), col(vr, RET_W), col(sg, RET_W),
                              state_ret.reshape(nseq, RET_QK, RET_DV), nseq=nseq, seqlen=lnew, odt=F32)
    conv_tm = lambda st: jnp.swapaxes(st, 0, 1).reshape(2 * nseq, 2 * D_FF)
    tmf = max(2 * nseq, _tile(rows, (512, 256, 128)))
    x2, cs0 = _layer_tail(x_tm, 0, gains, ffn, conv_tm(state_conv[0]), nb=nseq, tm=tmf,
                          mixer=(to_tm(o_mla).reshape(rows, MLA_W), o_ret.reshape(rows, RET_W), w0))
    u, q, k32, v32, _, _ = _inproj1(x2, gains[1, 0][None], w1["w_in1"], tm=tm, idt=F32)
    bb = _tile(nseq, (64, 32, 16, 8))
    o_s5, h_re, h_im = _s5(u.reshape(lnew, nseq, S5_CH), s5_re.reshape(nseq, S5_HID), s5_im.reshape(nseq, S5_HID),
                           ws5, tc=lnew, bb=bb, odt=F32)
    seq = lambda a: to_tm(a.reshape(lnew, nseq, SB_W))
    k_seq, v_seq = seq(k32), seq(v32)
    q_hd = seq(q).reshape(nseq, lnew, SB_HEADS, SB_DIM).transpose(0, 2, 1, 3)
    q_bd = jnp.einsum('bhtd,hg->bhtgd', q_hd, jnp.eye(SB_HEADS, dtype=F32)).reshape(nseq, SB_HEADS * lnew, SB_W)
    npage_sb = _tile(page_table.shape[1], (32, 16, 8, 4, 2))
    o_sb = _sb_sample(page_table, q_bd, pad_keys(k_seq), pad_keys(v_seq), jnp.transpose(cache_k, (0, 2, 3, 1)),
                      jnp.transpose(cache_v, (0, 2, 3, 1)), npage=npage_sb)
    o_sb_tm = to_tm(o_sb).reshape(rows, SB_W)
    x4, cs1 = _layer_tail(x2, 1, gains, ffn, conv_tm(state_conv[1]), nb=nseq, tm=tmf,
                          mixer=(o_s5.reshape(rows, S5_CH), o_sb_tm, w1))
    y = to_tm(x4.reshape(lnew, nseq, D_MODEL))
    conv = jnp.stack([cs0, cs1]).reshape(2, CONV_W - 1, nseq, 2 * D_FF).transpose(0, 2, 1, 3)
    return (y, rows_seq, s_new.reshape(nseq, RET_HEADS, RET_DK, RET_DV), h_re.reshape(nseq, S5_GROUPS, S5_STATE),
            h_im.reshape(nseq, S5_GROUPS, S5_STATE), k_seq.reshape(nseq, lnew, SB_HEADS, SB_DIM),
            v_seq.reshape(nseq, lnew, SB_HEADS, SB_DIM), conv)


def kernel(x_prompt, x_sample, cache_mla, state_ret, state_s5_re, state_s5_im, cache_sb_k, cache_sb_v,
           state_ffn_conv, page_table, norm_gains, w_in_l0, mla_q_norm, mla_w_uq, mla_kv_norm, mla_w_uk,
           mla_w_uv, w_out_l0, w_in_l1, s5_lam_re, s5_lam_im, s5_log_dt, s5_b_re, s5_b_im, s5_c_re, s5_c_im,
           s5_d, s5_w_glu, w_out_l1, ffn_w_up, ffn_conv_w, ffn_conv_b, ffn_w_down):
    w0 = _prep_layer0(w_in_l0, mla_q_norm, mla_w_uq, mla_kv_norm, mla_w_uk, mla_w_uv, w_out_l0)
    w1 = dict(w_in1=w_in_l1.astype(BF16), w_out_a=w_out_l1[:S5_CH].astype(BF16),
              w_out_b=w_out_l1[S5_CH:].astype(BF16))
    ws5 = _prep_s5(s5_lam_re, s5_lam_im, s5_log_dt, s5_b_re, s5_b_im, s5_c_re, s5_c_im, s5_d, s5_w_glu)
    ffn = dict(w_up=ffn_w_up.astype(BF16), conv_w=ffn_conv_w, conv_b=ffn_conv_b, w_down=ffn_w_down.astype(BF16))
    gains = norm_gains.astype(F32)
    (y_p, mla_p, ret_p, s5r_p, s5i_p, sbk_p, sbv_p, conv_p) = _run_prompt(x_prompt, gains, w0, w1, ws5, ffn)
    (y_s, mla_s, ret_s, s5r_s, s5i_s, sbk_s, sbv_s, conv_s) = _run_sample(
        x_sample, gains, w0, w1, ws5, ffn, cache_mla, state_ret, state_s5_re, state_s5_im, cache_sb_k, cache_sb_v,
        state_ffn_conv, page_table)
    return (y_p, y_s, mla_p, mla_s, ret_p, ret_s, s5r_p, s5i_p, s5r_s, s5i_s, sbk_p, sbv_p, sbk_s, sbv_s,
            conv_p, conv_s)
```
